```python
import math
import jax, jax.numpy as jnp
from jax import lax
import numpy as np

D_MODEL = 1024
BATCH = 8
SEQ = 4096
DEPTH = 1
DEC_BATCH = 16
DEC_SEQ = 2048
PAST_LEN = 128

GRID_W = 64
N_HEADS = 8
HEAD_DIM = 64
D_ATT = N_HEADS * HEAD_DIM
NA_KH = 8
NA_KW = 16
BAND_W = 2 * NA_KW
N_COL_BLOCKS = GRID_W // NA_KW
SSM_GROUP = 16
D_SSM = 512
N_GROUPS = D_SSM // SSM_GROUP
STATE_P = 64
DT_MIN = 1e-3
DT_MAX = 1e-1
D_FF = 2816
CONV_W = 3
D_IN = 3 * D_ATT + D_SSM + 2 * D_MODEL
EPS = 1e-6
NEG_BIG = -1e30

kernel_name = 'hybrid_natten_s5_encoder'


def rms_norm(x, g):
    x32 = x.astype(jnp.float32)
    y = x32 * lax.rsqrt(jnp.mean(x32 * x32, axis=-1, keepdims=True) + EPS) * g.astype(jnp.float32)
    return y.astype(x.dtype)


def neighborhood_attention(q, k, v, rpb):
    bsz, L = q.shape[0], q.shape[1]
    rows = L // GRID_W
    kh = min(NA_KH, rows)
    n_keys = kh * BAND_W
    scale = HEAD_DIM ** -0.5
    q5 = q.reshape(bsz, rows, GRID_W, N_HEADS, HEAD_DIM)
    k5 = k.reshape(bsz, rows, GRID_W, N_HEADS, HEAD_DIM)
    v5 = v.reshape(bsz, rows, GRID_W, N_HEADS, HEAD_DIM)
    cols = np.arange(GRID_W)
    col_start = np.clip(cols - NA_KW // 2, 0, GRID_W - NA_KW)
    band_start = np.clip(np.arange(N_COL_BLOCKS) * NA_KW - NA_KW // 2, 0, GRID_W - BAND_W)
    band_cols = band_start[:, None] + np.arange(BAND_W)
    qcol = cols.reshape(N_COL_BLOCKS, NA_KW)
    qstart = col_start[qcol][:, :, None]
    kc = band_cols[:, None, :]
    col_ok = (kc >= qstart) & (kc < qstart + NA_KW)
    dc_idx = np.clip(kc - qcol[:, :, None] + NA_KW - 1, 0, 2 * NA_KW - 2)
    mask = np.broadcast_to(col_ok[:, :, None, :], (N_COL_BLOCKS, NA_KW, kh, BAND_W)).reshape(
        N_COL_BLOCKS, NA_KW, n_keys)
    rpb32 = rpb.astype(jnp.float32)

    def one_row(args):
        r, q_r = args
        rs = jnp.clip(r - kh // 2, 0, rows - kh)
        k_r = lax.dynamic_slice_in_dim(k5, rs, kh, axis=1)
        v_r = lax.dynamic_slice_in_dim(v5, rs, kh, axis=1)
        k_b = jnp.moveaxis(k_r[:, :, band_cols], 2, 1).reshape(bsz, N_COL_BLOCKS, n_keys, N_HEADS, HEAD_DIM)
        v_b = jnp.moveaxis(v_r[:, :, band_cols], 2, 1).reshape(bsz, N_COL_BLOCKS, n_keys, N_HEADS, HEAD_DIM)
        q_b = q_r.reshape(bsz, N_COL_BLOCKS, NA_KW, N_HEADS, HEAD_DIM)
        s = jnp.einsum('bnqhd,bnkhd->bhnqk', q_b, k_b,
                       preferred_element_type=jnp.float32) * scale
        dr = rs + jnp.arange(kh) - r
        bias = rpb32[:, dr + NA_KH - 1][:, :, dc_idx]
        bias = jnp.transpose(bias, (0, 2, 3, 1, 4)).reshape(N_HEADS, N_COL_BLOCKS, NA_KW, n_keys)
        s = jnp.where(mask, s + bias, NEG_BIG)
        p = jax.nn.softmax(s, axis=-1)
        o = jnp.einsum('bhnqk,bnkhd->bnqhd', p.astype(v.dtype), v_b)
        return o.reshape(bsz, GRID_W, D_ATT)

    out = lax.map(one_row, (jnp.arange(rows), jnp.moveaxis(q5, 1, 0)))
    return jnp.moveaxis(out, 0, 1).reshape(bsz, L, D_ATT)


def _ssm_combine(left, right):
    a_l, b_l = left
    a_r, b_r = right
    return a_r * a_l, a_r * b_l + b_r


def s5_mixer(u, lam_re, lam_im, log_dt, b_re, b_im, c_re, c_im, d_skip, w_glu, b_glu):
    bsz, L = u.shape[0], u.shape[1]
    f32 = jnp.float32
    u32 = u.astype(f32).reshape(bsz, L, N_GROUPS, SSM_GROUP)
    uc = u32.astype(jnp.complex64)
    y = d_skip.astype(f32).reshape(N_GROUPS, SSM_GROUP) * u32
    for dirn, rev in enumerate((False, True)):
        lam = lax.complex(lam_re[dirn].astype(f32), lam_im[dirn].astype(f32))
        dt = jnp.exp(log_dt[dirn].astype(f32))[:, None]
        lam_bar = jnp.exp(lam * dt)
        bmat = lax.complex(b_re[dirn].astype(f32), b_im[dirn].astype(f32))
        b_bar = ((lam_bar - 1.0) / lam)[:, :, None] * bmat
        bu = jnp.einsum('blgh,gph->blgp', uc, b_bar)
        a = jnp.broadcast_to(lam_bar, bu.shape)
        _, h = lax.associative_scan(_ssm_combine, (a, bu), reverse=rev, axis=1)
        cmat = lax.complex(c_re[dirn].astype(f32), c_im[dirn].astype(f32))
        y = y + jnp.real(jnp.einsum('ghp,blgp->blgh', cmat, h))
    y = jax.nn.gelu(y.reshape(bsz, L, D_SSM))
    y = y * jax.nn.sigmoid(y @ w_glu.astype(f32) + b_glu.astype(f32))
    return y.astype(u.dtype)


def dwconv3_centred(x, w, b):
    xp = jnp.pad(x, ((0, 0), (1, 1), (0, 0)))
    return xp[:, :-2] * w[0] + xp[:, 1:-1] * w[1] + xp[:, 2:] * w[2] + b


def encoder_layer(x, g_mix_pre, g_mix_post, w_in, attn_rpb, ssm_lam_re, ssm_lam_im, ssm_log_dt,
                  ssm_b_re, ssm_b_im, ssm_c_re, ssm_c_im, ssm_d, w_glu, b_glu, w_branch_att,
                  w_branch_ssm, w_out, g_ffn_pre, g_ffn_post, w_up, conv_w, conv_b, w_down):
    bsz, L = x.shape[0], x.shape[1]
    h = rms_norm(x, g_mix_pre)
    proj = h @ w_in
    splits = np.cumsum([D_ATT, D_ATT, D_ATT, D_SSM, D_MODEL])
    q, k, v, u_ssm, gate_att, gate_ssm = jnp.split(proj, splits, axis=-1)
    q = q.reshape(bsz, L, N_HEADS, HEAD_DIM)
    k = k.reshape(bsz, L, N_HEADS, HEAD_DIM)
    v = v.reshape(bsz, L, N_HEADS, HEAD_DIM)
    att = neighborhood_attention(q, k, v, attn_rpb)
    ssm = s5_mixer(u_ssm, ssm_lam_re, ssm_lam_im, ssm_log_dt, ssm_b_re, ssm_b_im,
                   ssm_c_re, ssm_c_im, ssm_d, w_glu, b_glu)
    merged = (jax.nn.sigmoid(gate_att) * (att @ w_branch_att)
              + jax.nn.sigmoid(gate_ssm) * (ssm @ w_branch_ssm))
    x = x + rms_norm(merged @ w_out, g_mix_post)
    h = rms_norm(x, g_ffn_pre)
    up = dwconv3_centred(h @ w_up, conv_w, conv_b)
    a, b = jnp.split(up, 2, axis=-1)
    x = x + rms_norm((jax.nn.gelu(a) * b) @ w_down, g_ffn_post)
    return x


def setup_inputs(seed: int = 0) -> dict:
    key = jax.random.key(seed)
    ks = jax.random.split(key, 32)
    f32 = jnp.float32

    def nrm(k, shape, scale):
        return jax.random.normal(k, shape, f32) * scale

    n = jnp.arange(STATE_P, dtype=f32)
    return {
        'x_prompt': nrm(ks[0], (BATCH, SEQ, D_MODEL), 1.0),
        'x_sample': nrm(ks[1], (DEC_BATCH, DEC_SEQ, D_MODEL), 1.0),
        'g_mix_pre': 1.0 + nrm(ks[2], (DEPTH, D_MODEL), 0.05),
        'g_mix_post': 1.0 + nrm(ks[3], (DEPTH, D_MODEL), 0.05),
        'w_in': nrm(ks[4], (DEPTH, D_MODEL, D_IN), D_MODEL ** -0.5),
        'attn_rpb': nrm(ks[5], (DEPTH, N_HEADS, 2 * NA_KH - 1, 2 * NA_KW - 1), 0.02),
        'ssm_lam_re': -0.5 + nrm(ks[6], (DEPTH, 2, N_GROUPS, STATE_P), 0.01),
        'ssm_lam_im': math.pi * n + nrm(ks[7], (DEPTH, 2, N_GROUPS, STATE_P), 0.01),
        'ssm_log_dt': jax.random.uniform(ks[8], (DEPTH, 2, N_GROUPS), f32,
                                         minval=math.log(DT_MIN), maxval=math.log(DT_MAX)),
        'ssm_b_re': nrm(ks[9], (DEPTH, 2, N_GROUPS, STATE_P, SSM_GROUP), (2 * SSM_GROUP) ** -0.5),
        'ssm_b_im': nrm(ks[10], (DEPTH, 2, N_GROUPS, STATE_P, SSM_GROUP), (2 * SSM_GROUP) ** -0.5),
        'ssm_c_re': nrm(ks[11], (DEPTH, 2, N_GROUPS, SSM_GROUP, STATE_P), (2 * STATE_P) ** -0.5),
        'ssm_c_im': nrm(ks[12], (DEPTH, 2, N_GROUPS, SSM_GROUP, STATE_P), (2 * STATE_P) ** -0.5),
        'ssm_d': nrm(ks[13], (DEPTH, D_SSM), 1.0),
        'w_glu': nrm(ks[14], (DEPTH, D_SSM, D_SSM), D_SSM ** -0.5),
        'b_glu': nrm(ks[15], (DEPTH, D_SSM), 0.01),
        'w_branch_att': nrm(ks[16], (DEPTH, D_ATT, D_MODEL), D_ATT ** -0.5),
        'w_branch_ssm': nrm(ks[17], (DEPTH, D_SSM, D_MODEL), D_SSM ** -0.5),
        'w_out': nrm(ks[18], (DEPTH, D_MODEL, D_MODEL), D_MODEL ** -0.5),
        'g_ffn_pre': 1.0 + nrm(ks[19], (DEPTH, D_MODEL), 0.05),
        'g_ffn_post': 1.0 + nrm(ks[20], (DEPTH, D_MODEL), 0.05),
        'w_up': nrm(ks[21], (DEPTH, D_MODEL, 2 * D_FF), D_MODEL ** -0.5),
        'conv_w': nrm(ks[22], (DEPTH, CONV_W, 2 * D_FF), CONV_W ** -0.5),
        'conv_b': nrm(ks[23], (DEPTH, 2 * D_FF), 0.01),
        'w_down': nrm(ks[24], (DEPTH, D_FF, D_MODEL), D_FF ** -0.5),
    }


def reference(x_prompt, x_sample, g_mix_pre, g_mix_post, w_in, attn_rpb, ssm_lam_re, ssm_lam_im,
              ssm_log_dt, ssm_b_re, ssm_b_im, ssm_c_re, ssm_c_im, ssm_d, w_glu, b_glu, w_branch_att,
              w_branch_ssm, w_out, g_ffn_pre, g_ffn_post, w_up, conv_w, conv_b, w_down):
    def trunk(x):
        for l in range(DEPTH):
            x = encoder_layer(x, g_mix_pre[l], g_mix_post[l], w_in[l], attn_rpb[l], ssm_lam_re[l],
                              ssm_lam_im[l], ssm_log_dt[l], ssm_b_re[l], ssm_b_im[l], ssm_c_re[l],
                              ssm_c_im[l], ssm_d[l], w_glu[l], b_glu[l], w_branch_att[l],
                              w_branch_ssm[l], w_out[l], g_ffn_pre[l], g_ffn_post[l], w_up[l],
                              conv_w[l], conv_b[l], w_down[l])
        return x

    y_prompt = trunk(x_prompt)
    y_sample = trunk(x_sample)
    return (y_prompt, y_sample)
```

```python
import functools

import numpy as np
import jax
import jax.numpy as jnp
from jax import lax
from jax.experimental import pallas as pl
from jax.experimental.pallas import tpu as pltpu

D_MODEL = 1024
GRID_W = 64
N_HEADS = 8
HEAD_DIM = 64
D_ATT = N_HEADS * HEAD_DIM
NA_KH = 8
NA_KW = 16
SSM_GROUP = 16
D_SSM = 512
N_GROUPS = D_SSM // SSM_GROUP
STATE_P = 64
D_FF = 2816
D_IN = 3 * D_ATT + D_SSM + 2 * D_MODEL
EPS = 1e-6
NEG_BIG = -1e30

CHUNK = 16
FLAT = CHUNK * SSM_GROUP
Q_ROWS = 4
K_ROWS = 12
LANES = 128
VMEM_LIMIT = 56 * 1024 * 1024

F32 = jnp.float32
BF16 = jnp.bfloat16


def _rms(x, g):
    return x * lax.rsqrt(jnp.mean(x * x, axis=-1, keepdims=True) + EPS) * g


def _block_transpose8(vs):
    lane = lax.broadcasted_iota(jnp.int32, vs[0].shape, 1)
    vs = list(vs)
    for d in (4, 2, 1):
        keep = (lane & (d * SSM_GROUP)) == 0
        for r in range(8):
            if r & d:
                continue
            lo, hi = vs[r], vs[r + d]
            vs[r] = jnp.where(keep, lo, pltpu.roll(hi, d * SSM_GROUP, 1))
            vs[r + d] = jnp.where(keep, pltpu.roll(lo, LANES - d * SSM_GROUP, 1), hi)
    return vs


def _inproj_kernel(x_ref, g_ref, w_ref, q_ref, k_ref, v_ref, u_ref, uf_ref, ga_ref, gs_ref, u_sc, *, tm):
    x = x_ref[0]
    h = _rms(x, g_ref[...]).astype(BF16)

    def proj(c0, n):
        return jnp.dot(h, w_ref[:, c0:c0 + n], preferred_element_type=F32)

    q_ref[0] = (proj(0, D_ATT) * (HEAD_DIM ** -0.5)).astype(BF16)
    k_ref[0] = proj(D_ATT, D_ATT).astype(BF16)
    v_ref[0] = proj(2 * D_ATT, D_ATT).astype(BF16)
    u = proj(3 * D_ATT, D_SSM)
    u_ref[0] = u
    for jt in range(D_SSM // LANES):
        u_sc[jt] = u[:, jt * LANES:(jt + 1) * LANES]
    ga_ref[0] = proj(3 * D_ATT + D_SSM, D_MODEL)
    gs_ref[0] = proj(3 * D_ATT + D_SSM + D_MODEL, D_MODEL)

    n_rb = tm // (8 * CHUNK)
    pieces = [[[None, None] for _ in range(n_rb)] for _ in range(N_GROUPS)]
    for rb in range(n_rb):
        for jt in range(D_SSM // LANES):
            for half in range(2):
                vs = [u_sc[jt, pl.ds(rb * 8 * CHUNK + 8 * half + sl, 8, stride=CHUNK), :]
                      for sl in range(8)]
                out = _block_transpose8(vs)
                for gl in range(8):
                    pieces[8 * jt + gl][rb][half] = out[gl]
    for g in range(N_GROUPS):
        rows = [jnp.concatenate(pieces[g][rb], axis=1) for rb in range(n_rb)]
        uf_ref[0, g] = jnp.concatenate(rows, axis=0).astype(BF16)


def _inproj(x, g_pre, w_in, *, tm=512):
    B, L, _ = x.shape
    nc = L // CHUNK
    row = lambda n: pl.BlockSpec((1, tm, n), lambda b, i: (b, i, 0))
    const = lambda shape: pl.BlockSpec(shape, lambda b, i: (0,) * len(shape),
                                       pipeline_mode=pl.Buffered(1))
    return pl.pallas_call(
        functools.partial(_inproj_kernel, tm=tm),
        grid=(B, L // tm),
        in_specs=[row(D_MODEL), const((1, D_MODEL)), const((D_MODEL, D_IN))],
        out_specs=[row(D_ATT), row(D_ATT), row(D_ATT), row(D_SSM),
                   pl.BlockSpec((1, N_GROUPS, tm // CHUNK, FLAT), lambda b, i: (b, 0, i, 0)),
                   row(D_MODEL), row(D_MODEL)],
        out_shape=[jax.ShapeDtypeStruct((B, L, D_ATT), BF16)] * 3
        + [jax.ShapeDtypeStruct((B, L, D_SSM), F32),
           jax.ShapeDtypeStruct((B, N_GROUPS, nc, FLAT), BF16),
           jax.ShapeDtypeStruct((B, L, D_MODEL), F32),
           jax.ShapeDtypeStruct((B, L, D_MODEL), F32)],
        scratch_shapes=[pltpu.VMEM((D_SSM // LANES, tm, LANES), F32)],
        compiler_params=pltpu.CompilerParams(
            dimension_semantics=("arbitrary", "arbitrary"), vmem_limit_bytes=VMEM_LIMIT),
        name="inproj",
    )(x, g_pre, w_in)


def _attn_bias(rpb):
    i_rel = np.arange(Q_ROWS)[:, None]
    j_rel = np.arange(K_ROWS)[None, :]
    qc = np.arange(GRID_W)[:, None]
    kc = np.arange(GRID_W)[None, :]
    col_start = np.clip(qc - NA_KW // 2, 0, GRID_W - NA_KW)
    col_ok = (kc >= col_start) & (kc < col_start + NA_KW)
    dc = np.clip(kc - qc + NA_KW - 1, 0, 2 * NA_KW - 2)
    out = []
    for off, rs_rel in ((0, 0 * i_rel), (4, i_rel), (8, 0 * i_rel + 4)):
        dr = j_rel - i_rel - off
        row_ok = (j_rel >= rs_rel) & (j_rel < rs_rel + NA_KH)
        dr_idx = np.clip(dr + NA_KH - 1, 0, 2 * NA_KH - 2)
        bias = rpb[:, dr_idx[:, None, :, None], dc[None, :, None, :]]
        ok = row_ok[:, None, :, None] & col_ok[None, :, None, :]
        bias = jnp.where(ok[None], bias, NEG_BIG)
        out.append(bias.reshape(N_HEADS, Q_ROWS * GRID_W, K_ROWS * GRID_W))
    return jnp.stack(out).astype(F32)


def _attn_kernel(q_ref, k_ref, v_ref, bias_ref, o_ref, *, rows):
    g = pl.program_id(1)
    n_groups = rows // Q_ROWS
    kr0 = jnp.clip(g * Q_ROWS - NA_KH // 2, 0, rows - K_ROWS)
    pat = jnp.where(g == 0, 0, jnp.where(g == n_groups - 1, 2, 1))
    start = pl.multiple_of(kr0 * GRID_W, Q_ROWS * GRID_W)
    nq, nk = Q_ROWS * GRID_W, K_ROWS * GRID_W
    lane = lax.broadcasted_iota(jnp.int32, (nq, LANES), 1)
    for hp in range(N_HEADS // 2):
        cols = slice(hp * LANES, (hp + 1) * LANES)
        qp = q_ref[0, :, cols]
        kp = k_ref[0, pl.ds(start, nk), cols]
        vp = v_ref[0, pl.ds(start, nk), cols]
        outs = []
        for hh in range(2):
            in_head = (lane >= HEAD_DIM) if hh else (lane < HEAD_DIM)
            qm = jnp.where(in_head, qp, jnp.zeros_like(qp))
            s = lax.dot_general(qm, kp, (((1,), (1,)), ((), ())), preferred_element_type=F32)
            s = s + bias_ref[pat, 2 * hp + hh]
            m = jnp.max(s, axis=-1, keepdims=True)
            p = jnp.exp(s - m)
            l = jnp.sum(p, axis=-1, keepdims=True)
            o = jnp.dot(p.astype(BF16), vp, preferred_element_type=F32)
            outs.append(o / l)
        o_ref[0, :, cols] = jnp.where(lane < HEAD_DIM, outs[0], outs[1]).astype(BF16)


def _attention(q, k, v, bias):
    B, L, _ = q.shape
    rows = L // GRID_W
    assert rows % Q_ROWS == 0 and rows >= 3 * Q_ROWS
    nq = Q_ROWS * GRID_W
    seq = pl.BlockSpec((1, L, D_ATT), lambda b, g: (b, 0, 0))
    return pl.pallas_call(
        functools.partial(_attn_kernel, rows=rows),
        grid=(B, rows // Q_ROWS),
        in_specs=[pl.BlockSpec((1, nq, D_ATT), lambda b, g: (b, g, 0)), seq, seq,
                  pl.BlockSpec(bias.shape, lambda b, g: (0, 0, 0, 0), pipeline_mode=pl.Buffered(1))],
        out_specs=pl.BlockSpec((1, nq, D_ATT), lambda b, g: (b, g, 0)),
        out_shape=jax.ShapeDtypeStruct((B, L, D_ATT), BF16),
        compiler_params=pltpu.CompilerParams(
            dimension_semantics=("arbitrary", "arbitrary"), vmem_limit_bytes=VMEM_LIMIT),
        name="attn",
    )(q, k, v, bias)


def _cmul(ar, ai, br, bi):
    return ar * br - ai * bi, ar * bi + ai * br


def _ssm_tables(lam_re, lam_im, log_dt, b_re, b_im, c_re, c_im):
    hp = lax.Precision.HIGHEST
    lam_re, lam_im = lam_re.astype(F32), lam_im.astype(F32)
    dt = jnp.exp(log_dt.astype(F32))[..., None]
    zr, zi = lam_re * dt, lam_im * dt
    d = jnp.arange(CHUNK + 1, dtype=F32)[:, None, None, None]
    mag = jnp.exp(zr[None] * d)
    pw_re, pw_im = mag * jnp.cos(zi[None] * d), mag * jnp.sin(zi[None] * d)
    nr, ni = pw_re[1] - 1.0, pw_im[1]
    den = lam_re * lam_re + lam_im * lam_im
    fr, fi = (nr * lam_re + ni * lam_im) / den, (ni * lam_re - nr * lam_im) / den
    bb_re, bb_im = _cmul(fr[..., None], fi[..., None], b_re.astype(F32), b_im.astype(F32))
    c_re, c_im = c_re.astype(F32), c_im.astype(F32)

    cp_re, cp_im = _cmul(c_re[None], c_im[None], pw_re[:CHUNK, :, :, None, :], pw_im[:CHUNK, :, :, None, :])
    kern = (jnp.einsum('dzghp,zgpk->dzghk', cp_re, bb_re, precision=hp)
            - jnp.einsum('dzghp,zgpk->dzghk', cp_im, bb_im, precision=hp))
    kf, kb = kern[:, 0], kern[:, 1]
    k_all = jnp.concatenate([kb[:0:-1], (kf[0] + kb[0])[None], kf[1:]], axis=0)
    s_idx = np.arange(CHUNK)[:, None]
    j_idx = np.arange(CHUNK)[None, :]
    m = k_all[j_idx - s_idx + CHUNK - 1]
    m = jnp.transpose(m, (2, 0, 4, 1, 3)).reshape(N_GROUPS, FLAT, FLAT)

    def state_in(z, powers):
        pr, pi = _cmul(pw_re[powers, z][:, :, :, None], pw_im[powers, z][:, :, :, None],
                       bb_re[z][None], bb_im[z][None])
        both = jnp.concatenate([pr, pi], axis=2)
        return jnp.transpose(both, (1, 0, 3, 2)).reshape(N_GROUPS, FLAT, 2 * STATE_P)

    def state_out(z, powers):
        qr, qi = _cmul(c_re[z][None], c_im[z][None],
                       pw_re[powers, z][:, :, None, :], pw_im[powers, z][:, :, None, :])
        both = jnp.concatenate([qr, -qi], axis=3)
        return jnp.transpose(both, (1, 3, 0, 2)).reshape(N_GROUPS, 2 * STATE_P, FLAT)

    fwd = np.arange(CHUNK)
    pp = jnp.concatenate([state_in(0, CHUNK - 1 - fwd), state_in(1, fwd)], axis=2)
    mq = jnp.concatenate([m, state_out(0, fwd + 1), state_out(1, CHUNK - fwd)], axis=1)
    ar, ai = pw_re[CHUNK], pw_im[CHUNK]
    ca = jnp.concatenate([ar, ar], axis=-1)
    cb = jnp.concatenate([-ai, ai], axis=-1)
    coef = jnp.stack([ca[0], cb[0], ca[1], cb[1]])
    return pp.astype(BF16), mq.astype(BF16), coef


def _ssm_kernel(uf_ref, pp_ref, mq_ref, coef_ref, y_ref, s_ref, *, nc):
    half = 2 * STATE_P
    for g in range(N_GROUPS):
        ds = jnp.dot(uf_ref[0, g], pp_ref[g], preferred_element_type=F32)
        s_ref[0, pl.ds(g, nc, stride=N_GROUPS), :] = ds[:, :half]
        s_ref[1, pl.ds(g, nc, stride=N_GROUPS), :] = ds[:, half:]

    caf, cbf, cab, cbb = coef_ref[0], coef_ref[1], coef_ref[2], coef_ref[3]

    def step(c, carry):
        sf, sb = carry
        rf = pl.multiple_of(c * N_GROUPS, N_GROUPS)
        rb = pl.multiple_of((nc - 1 - c) * N_GROUPS, N_GROUPS)
        df = s_ref[0, pl.ds(rf, N_GROUPS), :]
        db = s_ref[1, pl.ds(rb, N_GROUPS), :]
        s_ref[0, pl.ds(rf, N_GROUPS), :] = sf
        s_ref[1, pl.ds(rb, N_GROUPS), :] = sb
        sf = caf * sf + cbf * pltpu.roll(sf, STATE_P, 1) + df
        sb = cab * sb + cbb * pltpu.roll(sb, STATE_P, 1) + db
        return sf, sb

    zero = jnp.zeros((N_GROUPS, half), F32)
    lax.fori_loop(0, nc, step, (zero, zero))

    for g in range(N_GROUPS):
        sf = s_ref[0, pl.ds(g, nc, stride=N_GROUPS), :].astype(BF16)
        sb = s_ref[1, pl.ds(g, nc, stride=N_GROUPS), :].astype(BF16)
        lhs = jnp.concatenate([uf_ref[0, g], sf, sb], axis=1)
        y_ref[0, g] = jnp.dot(lhs, mq_ref[g], preferred_element_type=F32)


def _ssm(uf, pp, mq, coef):
    B, _, nc, _ = uf.shape
    const = lambda a: pl.BlockSpec(a.shape, lambda b: (0,) * a.ndim, pipeline_mode=pl.Buffered(1))
    blk = pl.BlockSpec((1, N_GROUPS, nc, FLAT), lambda b: (b, 0, 0, 0))
    return pl.pallas_call(
        functools.partial(_ssm_kernel, nc=nc),
        grid=(B,),
        in_specs=[blk, const(pp), const(mq), const(coef)],
        out_specs=blk,
        out_shape=jax.ShapeDtypeStruct((B, N_GROUPS, nc, FLAT), F32),
        scratch_shapes=[pltpu.VMEM((2, nc * N_GROUPS, 2 * STATE_P), F32)],
        compiler_params=pltpu.CompilerParams(
            dimension_semantics=("arbitrary",), vmem_limit_bytes=VMEM_LIMIT),
        name="ssm",
    )(uf, pp, mq, coef)


def _merge_kernel(x_ref, att_ref, yf_ref, u_ref, ga_ref, gs_ref, d_ref, wglu_ref, bglu_ref,
                  watt_ref, wssm_ref, wout_ref, gpost_ref, o_ref, y_sc, *, tm):
    for rb in range(tm // (8 * CHUNK)):
        for jt in range(D_SSM // LANES):
            for half in range(2):
                vs = [yf_ref[0, 8 * jt + gl, rb * 8:(rb + 1) * 8, half * LANES:(half + 1) * LANES]
                      for gl in range(8)]
                out = _block_transpose8(vs)
                for sl in range(8):
                    y_sc[jt, pl.ds(rb * 8 * CHUNK + 8 * half + sl, 8, stride=CHUNK), :] = out[sl]

    y = jnp.concatenate([y_sc[jt] for jt in range(D_SSM // LANES)], axis=1)
    y = jax.nn.gelu(y + d_ref[...] * u_ref[0])
    z = jnp.dot(y.astype(BF16), wglu_ref[...], preferred_element_type=F32) + bglu_ref[...]
    ssm = (y * jax.nn.sigmoid(z)).astype(BF16)
    a = jnp.dot(att_ref[0], watt_ref[...], preferred_element_type=F32)
    b = jnp.dot(ssm, wssm_ref[...], preferred_element_type=F32)
    merged = jax.nn.sigmoid(ga_ref[0]) * a + jax.nn.sigmoid(gs_ref[0]) * b
    mo = jnp.dot(merged.astype(BF16), wout_ref[...], preferred_element_type=F32)
    o_ref[0] = x_ref[0] + _rms(mo, gpost_ref[...])


def _merge(x, att, yf, u, ga, gs, d_skip, w_glu, b_glu, w_att, w_ssm, w_out, g_post, *, tm=256):
    B, L, _ = x.shape
    row = lambda n: pl.BlockSpec((1, tm, n), lambda b, i: (b, i, 0))
    const = lambda a: pl.BlockSpec(a.shape, lambda b, i: (0,) * a.ndim, pipeline_mode=pl.Buffered(1))
    consts = (d_skip, w_glu, b_glu, w_att, w_ssm, w_out, g_post)
    return pl.pallas_call(
        functools.partial(_merge_kernel, tm=tm),
        grid=(B, L // tm),
        in_specs=[row(D_MODEL), row(D_ATT),
                  pl.BlockSpec((1, N_GROUPS, tm // CHUNK, FLAT), lambda b, i: (b, 0, i, 0)),
                  row(D_SSM), row(D_MODEL), row(D_MODEL)] + [const(a) for a in consts],
        out_specs=row(D_MODEL),
        out_shape=jax.ShapeDtypeStruct((B, L, D_MODEL), F32),
        scratch_shapes=[pltpu.VMEM((D_SSM // LANES, tm, LANES), F32)],
        compiler_params=pltpu.CompilerParams(
            dimension_semantics=("arbitrary", "arbitrary"), vmem_limit_bytes=VMEM_LIMIT),
        name="merge",
    )(x, att, yf, u, ga, gs, *consts)


HALO = 8
FF_CHUNK = 256


def _ffn_kernel(xp_ref, x_ref, xn_ref, gpre_ref, wup_ref, cw_ref, cb_ref, wdn_ref, gpost_ref, o_ref,
                *, tm, n_tiles):
    i = pl.program_id(1)
    x = x_ref[0]
    xp = jnp.where(i == 0, 0.0, xp_ref[0])
    xn = jnp.where(i == n_tiles - 1, 0.0, xn_ref[0])
    h = _rms(jnp.concatenate([xp, x, xn], axis=0), gpre_ref[...]).astype(BF16)
    n_ext = tm + 2 * HALO

    def conv(c0):
        up = jnp.dot(h, wup_ref[:, c0:c0 + FF_CHUNK], preferred_element_type=F32)
        w = cw_ref[:, c0:c0 + FF_CHUNK]
        prev = pltpu.roll(up, 1, 0)
        nxt = pltpu.roll(up, n_ext - 1, 0)
        r = prev * w[0:1] + up * w[1:2] + nxt * w[2:3] + cb_ref[:, c0:c0 + FF_CHUNK]
        return r[HALO:HALO + tm]

    acc = jnp.zeros((tm, D_MODEL), F32)
    for c in range(D_FF // FF_CHUNK):
        a = conv(c * FF_CHUNK)
        b = conv(D_FF + c * FF_CHUNK)
        act = (jax.nn.gelu(a) * b).astype(BF16)
        acc = acc + jnp.dot(act, wdn_ref[c * FF_CHUNK:(c + 1) * FF_CHUNK, :], preferred_element_type=F32)
    o_ref[0] = x + _rms(acc, gpost_ref[...])


def _ffn(x, g_pre, w_up, conv_w, conv_b, w_down, g_post, *, tm=512):
    B, L, _ = x.shape
    n_tiles = L // tm
    hb = tm // HALO
    const = lambda a: pl.BlockSpec(a.shape, lambda b, i: (0,) * a.ndim, pipeline_mode=pl.Buffered(1))
    consts_a = (g_pre, w_up, conv_w, conv_b, w_down, g_post)
    return pl.pallas_call(
        functools.partial(_ffn_kernel, tm=tm, n_tiles=n_tiles),
        grid=(B, n_tiles),
        in_specs=[pl.BlockSpec((1, HALO, D_MODEL), lambda b, i: (b, jnp.maximum(i * hb - 1, 0), 0)),
                  pl.BlockSpec((1, tm, D_MODEL), lambda b, i: (b, i, 0)),
                  pl.BlockSpec((1, HALO, D_MODEL),
                               lambda b, i: (b, jnp.minimum((i + 1) * hb, L // HALO - 1), 0))]
        + [const(a) for a in consts_a],
        out_specs=pl.BlockSpec((1, tm, D_MODEL), lambda b, i: (b, i, 0)),
        out_shape=jax.ShapeDtypeStruct((B, L, D_MODEL), F32),
        compiler_params=pltpu.CompilerParams(
            dimension_semantics=("arbitrary", "arbitrary"), vmem_limit_bytes=VMEM_LIMIT),
        name="ffn",
    )(x, x, x, *consts_a)


def _layer(x, p):
    q, k, v, u, uf, ga, gs = _inproj(x, p['g_mix_pre'], p['w_in'])
    att = _attention(q, k, v, p['attn_bias'])
    yf = _ssm(uf, p['pp'], p['mq'], p['coef'])
    x1 = _merge(x, att, yf, u, ga, gs, p['ssm_d'], p['w_glu'], p['b_glu'], p['w_branch_att'],
                p['w_branch_ssm'], p['w_out'], p['g_mix_post'])
    return _ffn(x1, p['g_ffn_pre'], p['w_up'], p['conv_w'], p['conv_b'], p['w_down'], p['g_ffn_post'])


def _prepare(l, g_mix_pre, g_mix_post, w_in, attn_rpb, ssm_lam_re, ssm_lam_im, ssm_log_dt, ssm_b_re,
             ssm_b_im, ssm_c_re, ssm_c_im, ssm_d, w_glu, b_glu, w_branch_att, w_branch_ssm, w_out,
             g_ffn_pre, g_ffn_post, w_up, conv_w, conv_b, w_down):
    pp, mq, coef = _ssm_tables(ssm_lam_re[l], ssm_lam_im[l], ssm_log_dt[l], ssm_b_re[l], ssm_b_im[l],
                               ssm_c_re[l], ssm_c_im[l])
    vec = lambda a: a[l].astype(F32).reshape(1, -1)
    return dict(
        g_mix_pre=vec(g_mix_pre), g_mix_post=vec(g_mix_post), w_in=w_in[l].astype(BF16),
        attn_bias=_attn_bias(attn_rpb[l].astype(F32)), pp=pp, mq=mq, coef=coef, ssm_d=vec(ssm_d),
        w_glu=w_glu[l].astype(BF16), b_glu=vec(b_glu), w_branch_att=w_branch_att[l].astype(BF16),
        w_branch_ssm=w_branch_ssm[l].astype(BF16), w_out=w_out[l].astype(BF16),
        g_ffn_pre=vec(g_ffn_pre), g_ffn_post=vec(g_ffn_post), w_up=w_up[l].astype(BF16),
        conv_w=conv_w[l].astype(F32), conv_b=vec(conv_b), w_down=w_down[l].astype(BF16))


def kernel(x_prompt, x_sample, g_mix_pre, g_mix_post, w_in, attn_rpb, ssm_lam_re, ssm_lam_im, ssm_log_dt,
           ssm_b_re, ssm_b_im, ssm_c_re, ssm_c_im, ssm_d, w_glu, b_glu, w_branch_att, w_branch_ssm, w_out,
           g_ffn_pre, g_ffn_post, w_up, conv_w, conv_b, w_down):
    weights = (g_mix_pre, g_mix_post, w_in, attn_rpb, ssm_lam_re, ssm_lam_im, ssm_log_dt, ssm_b_re,
               ssm_b_im, ssm_c_re, ssm_c_im, ssm_d, w_glu, b_glu, w_branch_att, w_branch_ssm, w_out,
               g_ffn_pre, g_ffn_post, w_up, conv_w, conv_b, w_down)
    layers = [_prepare(l, *weights) for l in range(w_in.shape[0])]

    def trunk(x):
        for p in layers:
            x = _layer(x, p)
        return x

    return trunk(x_prompt), trunk(x_sample)
```

```python
import functools

import numpy as np
import jax
import jax.numpy as jnp
from jax import lax
from jax.experimental import pallas as pl
from jax.experimental.pallas import tpu as pltpu

D_MODEL = 1024
GRID_W = 64
N_HEADS = 8
HEAD_DIM = 64
D_ATT = N_HEADS * HEAD_DIM
NA_KH = 8
NA_KW = 16
SSM_GROUP = 16
D_SSM = 512
N_GROUPS = D_SSM // SSM_GROUP
STATE_P = 64
D_FF = 2816
D_IN = 3 * D_ATT + D_SSM + 2 * D_MODEL
EPS = 1e-6
NEG_BIG = -1e30

CHUNK = 16
FLAT = CHUNK * SSM_GROUP
Q_ROWS = 4
K_ROWS = 12
LANES = 128
VMEM_LIMIT = 56 * 1024 * 1024

F32 = jnp.float32
BF16 = jnp.bfloat16


def _rms(x, g):
    return x * lax.rsqrt(jnp.mean(x * x, axis=-1, keepdims=True) + EPS) * g


def _block_transpose8(vs):
    lane = lax.broadcasted_iota(jnp.int32, vs[0].shape, 1)
    vs = list(vs)
    for d in (4, 2, 1):
        keep = (lane & (d * SSM_GROUP)) == 0
        for r in range(8):
            if r & d:
                continue
            lo, hi = vs[r], vs[r + d]
            vs[r] = jnp.where(keep, lo, pltpu.roll(hi, d * SSM_GROUP, 1))
            vs[r + d] = jnp.where(keep, pltpu.roll(lo, LANES - d * SSM_GROUP, 1), hi)
    return vs


def _inproj_kernel(x_ref, g_ref, w_ref, q_ref, k_ref, v_ref, u_ref, uf_ref, ga_ref, gs_ref, u_sc, *, tm):
    x = x_ref[0]
    h = _rms(x, g_ref[...]).astype(BF16)

    def proj(c0, n):
        return jnp.dot(h, w_ref[:, c0:c0 + n], preferred_element_type=F32)

    q_ref[0] = (proj(0, D_ATT) * (HEAD_DIM ** -0.5)).astype(BF16)
    k_ref[0] = proj(D_ATT, D_ATT).astype(BF16)
    v_ref[0] = proj(2 * D_ATT, D_ATT).astype(BF16)
    u = proj(3 * D_ATT, D_SSM)
    u_ref[0] = u
    for jt in range(D_SSM // LANES):
        u_sc[jt] = u[:, jt * LANES:(jt + 1) * LANES]
    ga_ref[0] = proj(3 * D_ATT + D_SSM, D_MODEL)
    gs_ref[0] = proj(3 * D_ATT + D_SSM + D_MODEL, D_MODEL)

    n_rb = tm // (8 * CHUNK)
    pieces = [[[None, None] for _ in range(n_rb)] for _ in range(N_GROUPS)]
    for rb in range(n_rb):
        for jt in range(D_SSM // LANES):
            for half in range(2):
                vs = [u_sc[jt, pl.ds(rb * 8 * CHUNK + 8 * half + sl, 8, stride=CHUNK), :]
                      for sl in range(8)]
                out = _block_transpose8(vs)
                for gl in range(8):
                    pieces[8 * jt + gl][rb][half] = out[gl]
    for g in range(N_GROUPS):
        rows = [jnp.concatenate(pieces[g][rb], axis=1) for rb in range(n_rb)]
        uf_ref[0, g] = jnp.concatenate(rows, axis=0).astype(BF16)


def _inproj(x, g_pre, w_in, *, tm=512):
    B, L, _ = x.shape
    nc = L // CHUNK
    row = lambda n: pl.BlockSpec((1, tm, n), lambda b, i: (b, i, 0))
    const = lambda shape: pl.BlockSpec(shape, lambda b, i: (0,) * len(shape),
                                       pipeline_mode=pl.Buffered(1))
    return pl.pallas_call(
        functools.partial(_inproj_kernel, tm=tm),
        grid=(B, L // tm),
        in_specs=[row(D_MODEL), const((1, D_MODEL)), const((D_MODEL, D_IN))],
        out_specs=[row(D_ATT), row(D_ATT), row(D_ATT), row(D_SSM),
                   pl.BlockSpec((1, N_GROUPS, tm // CHUNK, FLAT), lambda b, i: (b, 0, i, 0)),
                   row(D_MODEL), row(D_MODEL)],
        out_shape=[jax.ShapeDtypeStruct((B, L, D_ATT), BF16)] * 3
        + [jax.ShapeDtypeStruct((B, L, D_SSM), F32),
           jax.ShapeDtypeStruct((B, N_GROUPS, nc, FLAT), BF16),
           jax.ShapeDtypeStruct((B, L, D_MODEL), F32),
           jax.ShapeDtypeStruct((B, L, D_MODEL), F32)],
        scratch_shapes=[pltpu.VMEM((D_SSM // LANES, tm, LANES), F32)],
        compiler_params=pltpu.CompilerParams(
            dimension_semantics=("arbitrary", "arbitrary"), vmem_limit_bytes=VMEM_LIMIT),
        name="inproj",
    )(x, g_pre, w_in)


def _attn_bias(rpb):
    qc = np.arange(GRID_W)[:, None]
    kc = np.arange(GRID_W)[None, :]
    col_start = np.clip(qc - NA_KW // 2, 0, GRID_W - NA_KW)
    col_ok = (kc >= col_start) & (kc < col_start + NA_KW)
    pad = GRID_W - NA_KW
    rp = jnp.pad(rpb, ((0, 0), (0, 0), (pad, pad)))
    tab = jnp.stack([rp[:, :, GRID_W - 1 - c:2 * GRID_W - 1 - c] for c in range(GRID_W)], axis=2)
    tab = jnp.where(col_ok[None, None], tab, NEG_BIG)
    neg = jnp.full((N_HEADS, GRID_W, GRID_W), NEG_BIG, F32)
    out = []
    for off, rs_rel in ((0, (0, 0, 0, 0)), (4, (0, 1, 2, 3)), (8, (4, 4, 4, 4))):
        rows = []
        for i in range(Q_ROWS):
            blocks = [tab[:, j - i - off + NA_KH - 1] if rs_rel[i] <= j < rs_rel[i] + NA_KH else neg
                      for j in range(K_ROWS)]
            rows.append(jnp.concatenate(blocks, axis=2))
        out.append(jnp.concatenate(rows, axis=1))
    return jnp.stack(out).astype(F32)


def _attn_kernel(q_ref, k_ref, v_ref, bias_ref, o_ref, *, rows):
    g = pl.program_id(1)
    n_groups = rows // Q_ROWS
    kr0 = jnp.clip(g * Q_ROWS - NA_KH // 2, 0, rows - K_ROWS)
    pat = jnp.where(g == 0, 0, jnp.where(g == n_groups - 1, 2, 1))
    start = pl.multiple_of(kr0 * GRID_W, Q_ROWS * GRID_W)
    nq, nk = Q_ROWS * GRID_W, K_ROWS * GRID_W
    lane = lax.broadcasted_iota(jnp.int32, (nq, LANES), 1)
    for hp in range(N_HEADS // 2):
        cols = slice(hp * LANES, (hp + 1) * LANES)
        qp = q_ref[0, :, cols]
        kp = k_ref[0, pl.ds(start, nk), cols]
        vp = v_ref[0, pl.ds(start, nk), cols]
        outs = []
        for hh in range(2):
            in_head = (lane >= HEAD_DIM) if hh else (lane < HEAD_DIM)
            qm = jnp.where(in_head, qp, jnp.zeros_like(qp))
            s = lax.dot_general(qm, kp, (((1,), (1,)), ((), ())), preferred_element_type=F32)
            s = s + bias_ref[pat, 2 * hp + hh]
            m = jnp.max(s, axis=-1, keepdims=True)
            p = jnp.exp(s - m)
            l = jnp.sum(p, axis=-1, keepdims=True)
            o = jnp.dot(p.astype(BF16), vp, preferred_element_type=F32)
            outs.append(o / l)
        o_ref[0, :, cols] = jnp.where(lane < HEAD_DIM, outs[0], outs[1]).astype(BF16)


def _attention(q, k, v, bias):
    B, L, _ = q.shape
    rows = L // GRID_W
    assert rows % Q_ROWS == 0 and rows >= 3 * Q_ROWS
    nq = Q_ROWS * GRID_W
    seq = pl.BlockSpec((1, L, D_ATT), lambda b, g: (b, 0, 0))
    return pl.pallas_call(
        functools.partial(_attn_kernel, rows=rows),
        grid=(B, rows // Q_ROWS),
        in_specs=[pl.BlockSpec((1, nq, D_ATT), lambda b, g: (b, g, 0)), seq, seq,
                  pl.BlockSpec(bias.shape, lambda b, g: (0, 0, 0, 0), pipeline_mode=pl.Buffered(1))],
        out_specs=pl.BlockSpec((1, nq, D_ATT), lambda b, g: (b, g, 0)),
        out_shape=jax.ShapeDtypeStruct((B, L, D_ATT), BF16),
        compiler_params=pltpu.CompilerParams(
            dimension_semantics=("arbitrary", "arbitrary"), vmem_limit_bytes=VMEM_LIMIT),
        name="attn",
    )(q, k, v, bias)


def _cmul(ar, ai, br, bi):
    return ar * br - ai * bi, ar * bi + ai * br


def _ssm_tables(lam_re, lam_im, log_dt, b_re, b_im, c_re, c_im):
    hp = lax.Precision.HIGHEST
    lam_re, lam_im = lam_re.astype(F32), lam_im.astype(F32)
    dt = jnp.exp(log_dt.astype(F32))[..., None]
    zr, zi = lam_re * dt, lam_im * dt
    d = jnp.arange(CHUNK + 1, dtype=F32)[:, None, None, None]
    mag = jnp.exp(zr[None] * d)
    pw_re, pw_im = mag * jnp.cos(zi[None] * d), mag * jnp.sin(zi[None] * d)
    nr, ni = pw_re[1] - 1.0, pw_im[1]
    den = lam_re * lam_re + lam_im * lam_im
    fr, fi = (nr * lam_re + ni * lam_im) / den, (ni * lam_re - nr * lam_im) / den
    bb_re, bb_im = _cmul(fr[..., None], fi[..., None], b_re.astype(F32), b_im.astype(F32))
    c_re, c_im = c_re.astype(F32), c_im.astype(F32)

    cp_re, cp_im = _cmul(c_re[None], c_im[None], pw_re[:CHUNK, :, :, None, :], pw_im[:CHUNK, :, :, None, :])
    kern = (jnp.einsum('dzghp,zgpk->dzghk', cp_re, bb_re, precision=hp)
            - jnp.einsum('dzghp,zgpk->dzghk', cp_im, bb_im, precision=hp))
    kf, kb = kern[:, 0], kern[:, 1]
    k_all = jnp.concatenate([kb[:0:-1], (kf[0] + kb[0])[None], kf[1:]], axis=0)
    s_idx = np.arange(CHUNK)[:, None]
    j_idx = np.arange(CHUNK)[None, :]
    m = k_all[j_idx - s_idx + CHUNK - 1]
    m = jnp.transpose(m, (2, 0, 4, 1, 3)).reshape(N_GROUPS, FLAT, FLAT)

    def state_in(z, powers):
        pr, pi = _cmul(pw_re[powers, z][:, :, :, None], pw_im[powers, z][:, :, :, None],
                       bb_re[z][None], bb_im[z][None])
        both = jnp.concatenate([pr, pi], axis=2)
        return jnp.transpose(both, (1, 0, 3, 2)).reshape(N_GROUPS, FLAT, 2 * STATE_P)

    def state_out(z, powers):
        qr, qi = _cmul(c_re[z][None], c_im[z][None],
                       pw_re[powers, z][:, :, None, :], pw_im[powers, z][:, :, None, :])
        both = jnp.concatenate([qr, -qi], axis=3)
        return jnp.transpose(both, (1, 3, 0, 2)).reshape(N_GROUPS, 2 * STATE_P, FLAT)

    fwd = np.arange(CHUNK)
    pp = jnp.concatenate([state_in(0, CHUNK - 1 - fwd), state_in(1, fwd)], axis=2)
    mq = jnp.concatenate([m, state_out(0, fwd + 1), state_out(1, CHUNK - fwd)], axis=1)
    ar, ai = pw_re[CHUNK], pw_im[CHUNK]
    ca = jnp.concatenate([ar, ar], axis=-1)
    cb = jnp.concatenate([-ai, ai], axis=-1)
    coef = jnp.stack([ca[0], cb[0], ca[1], cb[1]])
    return pp.astype(BF16), mq.astype(BF16), coef


def _ssm_kernel(uf_ref, pp_ref, mq_ref, coef_ref, y_ref, s_ref, *, nc):
    half = 2 * STATE_P
    for g in range(N_GROUPS):
        ds = jnp.dot(uf_ref[0, g], pp_ref[g], preferred_element_type=F32)
        s_ref[0, pl.ds(g, nc, stride=N_GROUPS), :] = ds[:, :half]
        s_ref[1, pl.ds(g, nc, stride=N_GROUPS), :] = ds[:, half:]

    caf, cbf, cab, cbb = coef_ref[0], coef_ref[1], coef_ref[2], coef_ref[3]

    def step(c, carry):
        sf, sfx, sb, sbx = carry
        rf = pl.multiple_of(c * N_GROUPS, N_GROUPS)
        rb = pl.multiple_of((nc - 1 - c) * N_GROUPS, N_GROUPS)
        df = s_ref[0, pl.ds(rf, N_GROUPS), :]
        db = s_ref[1, pl.ds(rb, N_GROUPS), :]
        s_ref[0, pl.ds(rf, N_GROUPS), :] = sf
        s_ref[1, pl.ds(rb, N_GROUPS), :] = sb
        dfx = pltpu.roll(df, STATE_P, 1)
        dbx = pltpu.roll(db, STATE_P, 1)
        sf, sfx = caf * sf + cbf * sfx + df, caf * sfx - cbf * sf + dfx
        sb, sbx = cab * sb + cbb * sbx + db, cab * sbx - cbb * sb + dbx
        return sf, sfx, sb, sbx

    zero = jnp.zeros((N_GROUPS, half), F32)
    lax.fori_loop(0, nc, step, (zero, zero, zero, zero), unroll=8)

    for g in range(N_GROUPS):
        sf = s_ref[0, pl.ds(g, nc, stride=N_GROUPS), :].astype(BF16)
        sb = s_ref[1, pl.ds(g, nc, stride=N_GROUPS), :].astype(BF16)
        lhs = jnp.concatenate([uf_ref[0, g], sf, sb], axis=1)
        y_ref[0, g] = jnp.dot(lhs, mq_ref[g], preferred_element_type=F32)


def _ssm(uf, pp, mq, coef):
    B, _, nc, _ = uf.shape
    const = lambda a: pl.BlockSpec(a.shape, lambda b: (0,) * a.ndim, pipeline_mode=pl.Buffered(1))
    blk = pl.BlockSpec((1, N_GROUPS, nc, FLAT), lambda b: (b, 0, 0, 0))
    return pl.pallas_call(
        functools.partial(_ssm_kernel, nc=nc),
        grid=(B,),
        in_specs=[blk, const(pp), const(mq), const(coef)],
        out_specs=blk,
        out_shape=jax.ShapeDtypeStruct((B, N_GROUPS, nc, FLAT), F32),
        scratch_shapes=[pltpu.VMEM((2, nc * N_GROUPS, 2 * STATE_P), F32)],
        compiler_params=pltpu.CompilerParams(
            dimension_semantics=("arbitrary",), vmem_limit_bytes=VMEM_LIMIT),
        name="ssm",
    )(uf, pp, mq, coef)


def _merge_kernel(x_ref, att_ref, yf_ref, u_ref, ga_ref, gs_ref, d_ref, wglu_ref, bglu_ref,
                  watt_ref, wssm_ref, wout_ref, gpost_ref, o_ref, y_sc, *, tm):
    for rb in range(tm // (8 * CHUNK)):
        for jt in range(D_SSM // LANES):
            for half in range(2):
                vs = [yf_ref[0, 8 * jt + gl, rb * 8:(rb + 1) * 8, half * LANES:(half + 1) * LANES]
                      for gl in range(8)]
                out = _block_transpose8(vs)
                for sl in range(8):
                    y_sc[jt, pl.ds(rb * 8 * CHUNK + 8 * half + sl, 8, stride=CHUNK), :] = out[sl]

    y = jnp.concatenate([y_sc[jt] for jt in range(D_SSM // LANES)], axis=1)
    y = jax.nn.gelu(y + d_ref[...] * u_ref[0])
    z = jnp.dot(y.astype(BF16), wglu_ref[...], preferred_element_type=F32) + bglu_ref[...]
    ssm = (y * jax.nn.sigmoid(z)).astype(BF16)
    a = jnp.dot(att_ref[0], watt_ref[...], preferred_element_type=F32)
    b = jnp.dot(ssm, wssm_ref[...], preferred_element_type=F32)
    merged = jax.nn.sigmoid(ga_ref[0]) * a + jax.nn.sigmoid(gs_ref[0]) * b
    mo = jnp.dot(merged.astype(BF16), wout_ref[...], preferred_element_type=F32)
    o_ref[0] = x_ref[0] + _rms(mo, gpost_ref[...])


def _merge(x, att, yf, u, ga, gs, d_skip, w_glu, b_glu, w_att, w_ssm, w_out, g_post, *, tm=256):
    B, L, _ = x.shape
    row = lambda n: pl.BlockSpec((1, tm, n), lambda b, i: (b, i, 0))
    const = lambda a: pl.BlockSpec(a.shape, lambda b, i: (0,) * a.ndim, pipeline_mode=pl.Buffered(1))
    consts = (d_skip, w_glu, b_glu, w_att, w_ssm, w_out, g_post)
    return pl.pallas_call(
        functools.partial(_merge_kernel, tm=tm),
        grid=(B, L // tm),
        in_specs=[row(D_MODEL), row(D_ATT),
                  pl.BlockSpec((1, N_GROUPS, tm // CHUNK, FLAT), lambda b, i: (b, 0, i, 0)),
                  row(D_SSM), row(D_MODEL), row(D_MODEL)] + [const(a) for a in consts],
        out_specs=row(D_MODEL),
        out_shape=jax.ShapeDtypeStruct((B, L, D_MODEL), F32),
        scratch_shapes=[pltpu.VMEM((D_SSM // LANES, tm, LANES), F32)],
        compiler_params=pltpu.CompilerParams(
            dimension_semantics=("arbitrary", "arbitrary"), vmem_limit_bytes=VMEM_LIMIT),
        name="merge",
    )(x, att, yf, u, ga, gs, *consts)


HALO = 8
FF_CHUNK = 256


def _ffn_kernel(xp_ref, x_ref, xn_ref, gpre_ref, wup_ref, cw_ref, cb_ref, wdn_ref, gpost_ref, o_ref,
                *, tm, n_tiles):
    i = pl.program_id(1)
    x = x_ref[0]
    xp = jnp.where(i == 0, 0.0, xp_ref[0])
    xn = jnp.where(i == n_tiles - 1, 0.0, xn_ref[0])
    h = _rms(jnp.concatenate([xp, x, xn], axis=0), gpre_ref[...]).astype(BF16)
    n_ext = tm + 2 * HALO

    def conv(c0):
        up = jnp.dot(h, wup_ref[:, c0:c0 + FF_CHUNK], preferred_element_type=F32)
        w = cw_ref[:, c0:c0 + FF_CHUNK]
        prev = pltpu.roll(up, 1, 0)
        nxt = pltpu.roll(up, n_ext - 1, 0)
        r = prev * w[0:1] + up * w[1:2] + nxt * w[2:3] + cb_ref[:, c0:c0 + FF_CHUNK]
        return r[HALO:HALO + tm]

    acc = jnp.zeros((tm, D_MODEL), F32)
    for c in range(D_FF // FF_CHUNK):
        a = conv(c * FF_CHUNK)
        b = conv(D_FF + c * FF_CHUNK)
        act = (jax.nn.gelu(a) * b).astype(BF16)
        acc = acc + jnp.dot(act, wdn_ref[c * FF_CHUNK:(c + 1) * FF_CHUNK, :], preferred_element_type=F32)
    o_ref[0] = x + _rms(acc, gpost_ref[...])


def _ffn(x, g_pre, w_up, conv_w, conv_b, w_down, g_post, *, tm=512):
    B, L, _ = x.shape
    n_tiles = L // tm
    hb = tm // HALO
    const = lambda a: pl.BlockSpec(a.shape, lambda b, i: (0,) * a.ndim, pipeline_mode=pl.Buffered(1))
    consts_a = (g_pre, w_up, conv_w, conv_b, w_down, g_post)
    return pl.pallas_call(
        functools.partial(_ffn_kernel, tm=tm, n_tiles=n_tiles),
        grid=(B, n_tiles),
        in_specs=[pl.BlockSpec((1, HALO, D_MODEL), lambda b, i: (b, jnp.maximum(i * hb - 1, 0), 0)),
                  pl.BlockSpec((1, tm, D_MODEL), lambda b, i: (b, i, 0)),
                  pl.BlockSpec((1, HALO, D_MODEL),
                               lambda b, i: (b, jnp.minimum((i + 1) * hb, L // HALO - 1), 0))]
        + [const(a) for a in consts_a],
        out_specs=pl.BlockSpec((1, tm, D_MODEL), lambda b, i: (b, i, 0)),
        out_shape=jax.ShapeDtypeStruct((B, L, D_MODEL), F32),
        compiler_params=pltpu.CompilerParams(
            dimension_semantics=("arbitrary", "arbitrary"), vmem_limit_bytes=VMEM_LIMIT),
        name="ffn",
    )(x, x, x, *consts_a)


def _layer(x, p):
    q, k, v, u, uf, ga, gs = _inproj(x, p['g_mix_pre'], p['w_in'])
    att = _attention(q, k, v, p['attn_bias'])
    yf = _ssm(uf, p['pp'], p['mq'], p['coef'])
    x1 = _merge(x, att, yf, u, ga, gs, p['ssm_d'], p['w_glu'], p['b_glu'], p['w_branch_att'],
                p['w_branch_ssm'], p['w_out'], p['g_mix_post'])
    return _ffn(x1, p['g_ffn_pre'], p['w_up'], p['conv_w'], p['conv_b'], p['w_down'], p['g_ffn_post'])


def _prepare(l, g_mix_pre, g_mix_post, w_in, attn_rpb, ssm_lam_re, ssm_lam_im, ssm_log_dt, ssm_b_re,
             ssm_b_im, ssm_c_re, ssm_c_im, ssm_d, w_glu, b_glu, w_branch_att, w_branch_ssm, w_out,
             g_ffn_pre, g_ffn_post, w_up, conv_w, conv_b, w_down):
    pp, mq, coef = _ssm_tables(ssm_lam_re[l], ssm_lam_im[l], ssm_log_dt[l], ssm_b_re[l], ssm_b_im[l],
                               ssm_c_re[l], ssm_c_im[l])
    vec = lambda a: a[l].astype(F32).reshape(1, -1)
    return dict(
        g_mix_pre=vec(g_mix_pre), g_mix_post=vec(g_mix_post), w_in=w_in[l].astype(BF16),
        attn_bias=_attn_bias(attn_rpb[l].astype(F32)), pp=pp, mq=mq, coef=coef, ssm_d=vec(ssm_d),
        w_glu=w_glu[l].astype(BF16), b_glu=vec(b_glu), w_branch_att=w_branch_att[l].astype(BF16),
        w_branch_ssm=w_branch_ssm[l].astype(BF16), w_out=w_out[l].astype(BF16),
        g_ffn_pre=vec(g_ffn_pre), g_ffn_post=vec(g_ffn_post), w_up=w_up[l].astype(BF16),
        conv_w=conv_w[l].astype(F32), conv_b=vec(conv_b), w_down=w_down[l].astype(BF16))


def kernel(x_prompt, x_sample, g_mix_pre, g_mix_post, w_in, attn_rpb, ssm_lam_re, ssm_lam_im, ssm_log_dt,
           ssm_b_re, ssm_b_im, ssm_c_re, ssm_c_im, ssm_d, w_glu, b_glu, w_branch_att, w_branch_ssm, w_out,
           g_ffn_pre, g_ffn_post, w_up, conv_w, conv_b, w_down):
    weights = (g_mix_pre, g_mix_post, w_in, attn_rpb, ssm_lam_re, ssm_lam_im, ssm_log_dt, ssm_b_re,
               ssm_b_im, ssm_c_re, ssm_c_im, ssm_d, w_glu, b_glu, w_branch_att, w_branch_ssm, w_out,
               g_ffn_pre, g_ffn_post, w_up, conv_w, conv_b, w_down)
    layers = [_prepare(l, *weights) for l in range(w_in.shape[0])]

    def trunk(x):
        for p in layers:
            x = _layer(x, p)
        return x

    return trunk(x_prompt), trunk(x_sample)
```

```python
import functools

import numpy as np
import jax
import jax.numpy as jnp
from jax import lax
from jax.experimental import pallas as pl
from jax.experimental.pallas import tpu as pltpu

D_MODEL = 1024
GRID_W = 64
N_HEADS = 8
HEAD_DIM = 64
D_ATT = N_HEADS * HEAD_DIM
NA_KH = 8
NA_KW = 16
SSM_GROUP = 16
D_SSM = 512
N_GROUPS = D_SSM // SSM_GROUP
STATE_P = 64
D_FF = 2816
D_IN = 3 * D_ATT + D_SSM + 2 * D_MODEL
EPS = 1e-6
NEG_BIG = -1e30

CHUNK = 16
FLAT = CHUNK * SSM_GROUP
Q_ROWS = 4
K_ROWS = 12
LANES = 128
VMEM_LIMIT = 56 * 1024 * 1024

F32 = jnp.float32
BF16 = jnp.bfloat16


def _rms(x, g):
    return x * lax.rsqrt(jnp.mean(x * x, axis=-1, keepdims=True) + EPS) * g


def _block_transpose8(vs):
    lane = lax.broadcasted_iota(jnp.int32, vs[0].shape, 1)
    vs = list(vs)
    for d in (4, 2, 1):
        keep = (lane & (d * SSM_GROUP)) == 0
        for r in range(8):
            if r & d:
                continue
            lo, hi = vs[r], vs[r + d]
            vs[r] = jnp.where(keep, lo, pltpu.roll(hi, d * SSM_GROUP, 1))
            vs[r + d] = jnp.where(keep, pltpu.roll(lo, LANES - d * SSM_GROUP, 1), hi)
    return vs


def _inproj_kernel(x_ref, g_ref, w_ref, q_ref, k_ref, v_ref, u_ref, uf_ref, ga_ref, gs_ref, u_sc, *, tm):
    x = x_ref[0]
    h = _rms(x, g_ref[...]).astype(BF16)

    def proj(c0, n):
        return jnp.dot(h, w_ref[:, c0:c0 + n], preferred_element_type=F32)

    q_ref[0] = (proj(0, D_ATT) * (HEAD_DIM ** -0.5)).astype(BF16)
    k_ref[0] = proj(D_ATT, D_ATT).astype(BF16)
    v_ref[0] = proj(2 * D_ATT, D_ATT).astype(BF16)
    u = proj(3 * D_ATT, D_SSM)
    u_ref[0] = u
    for jt in range(D_SSM // LANES):
        u_sc[jt] = u[:, jt * LANES:(jt + 1) * LANES]
    ga_ref[0] = proj(3 * D_ATT + D_SSM, D_MODEL)
    gs_ref[0] = proj(3 * D_ATT + D_SSM + D_MODEL, D_MODEL)

    n_rb = tm // (8 * CHUNK)
    pieces = [[[None, None] for _ in range(n_rb)] for _ in range(N_GROUPS)]
    for rb in range(n_rb):
        for jt in range(D_SSM // LANES):
            for half in range(2):
                vs = [u_sc[jt, pl.ds(rb * 8 * CHUNK + 8 * half + sl, 8, stride=CHUNK), :]
                      for sl in range(8)]
                out = _block_transpose8(vs)
                for gl in range(8):
                    pieces[8 * jt + gl][rb][half] = out[gl]
    for g in range(N_GROUPS):
        rows = [jnp.concatenate(pieces[g][rb], axis=1) for rb in range(n_rb)]
        uf_ref[0, g] = jnp.concatenate(rows, axis=0).astype(BF16)


def _inproj(x, g_pre, w_in, *, tm=512):
    B, L, _ = x.shape
    nc = L // CHUNK
    row = lambda n: pl.BlockSpec((1, tm, n), lambda b, i: (b, i, 0))
    const = lambda shape: pl.BlockSpec(shape, lambda b, i: (0,) * len(shape),
                                       pipeline_mode=pl.Buffered(1))
    return pl.pallas_call(
        functools.partial(_inproj_kernel, tm=tm),
        grid=(B, L // tm),
        in_specs=[row(D_MODEL), const((1, D_MODEL)), const((D_MODEL, D_IN))],
        out_specs=[row(D_ATT), row(D_ATT), row(D_ATT), row(D_SSM),
                   pl.BlockSpec((1, N_GROUPS, tm // CHUNK, FLAT), lambda b, i: (b, 0, i, 0)),
                   row(D_MODEL), row(D_MODEL)],
        out_shape=[jax.ShapeDtypeStruct((B, L, D_ATT), BF16)] * 3
        + [jax.ShapeDtypeStruct((B, L, D_SSM), F32),
           jax.ShapeDtypeStruct((B, N_GROUPS, nc, FLAT), BF16),
           jax.ShapeDtypeStruct((B, L, D_MODEL), F32),
           jax.ShapeDtypeStruct((B, L, D_MODEL), F32)],
        scratch_shapes=[pltpu.VMEM((D_SSM // LANES, tm, LANES), F32)],
        compiler_params=pltpu.CompilerParams(
            dimension_semantics=("arbitrary", "arbitrary"), vmem_limit_bytes=VMEM_LIMIT),
        name="inproj",
    )(x, g_pre, w_in)


def _attn_bias(rpb):
    qc = np.arange(GRID_W)[:, None]
    kc = np.arange(GRID_W)[None, :]
    col_start = np.clip(qc - NA_KW // 2, 0, GRID_W - NA_KW)
    col_ok = (kc >= col_start) & (kc < col_start + NA_KW)
    pad = GRID_W - NA_KW
    rp = jnp.pad(rpb, ((0, 0), (0, 0), (pad, pad)))
    tab = jnp.stack([rp[:, :, GRID_W - 1 - c:2 * GRID_W - 1 - c] for c in range(GRID_W)], axis=2)
    tab = jnp.where(col_ok[None, None], tab, NEG_BIG)
    neg = jnp.full((N_HEADS, GRID_W, GRID_W), NEG_BIG, F32)
    out = []
    for off, rs_rel in ((0, (0, 0, 0, 0)), (4, (0, 1, 2, 3)), (8, (4, 4, 4, 4))):
        rows = []
        for i in range(Q_ROWS):
            blocks = [tab[:, j - i - off + NA_KH - 1] if rs_rel[i] <= j < rs_rel[i] + NA_KH else neg
                      for j in range(K_ROWS)]
            rows.append(jnp.concatenate(blocks, axis=2))
        out.append(jnp.concatenate(rows, axis=1))
    return jnp.stack(out).astype(F32).reshape(3, N_HEADS // 2, 2 * Q_ROWS * GRID_W, K_ROWS * GRID_W)


def _attn_kernel(q_ref, k_ref, v_ref, bias_ref, o_ref, *, rows):
    g = pl.program_id(1)
    n_groups = rows // Q_ROWS
    kr0 = jnp.clip(g * Q_ROWS - NA_KH // 2, 0, rows - K_ROWS)
    pat = jnp.where(g == 0, 0, jnp.where(g == n_groups - 1, 2, 1))
    start = pl.multiple_of(kr0 * GRID_W, Q_ROWS * GRID_W)
    nq, nk = Q_ROWS * GRID_W, K_ROWS * GRID_W
    lane = lax.broadcasted_iota(jnp.int32, (nq, LANES), 1)
    for hp in range(N_HEADS // 2):
        cols = slice(hp * LANES, (hp + 1) * LANES)
        qp = q_ref[0, :, cols]
        kp = k_ref[0, pl.ds(start, nk), cols]
        vp = v_ref[0, pl.ds(start, nk), cols]
        q2 = jnp.concatenate([jnp.where(lane < HEAD_DIM, qp, jnp.zeros_like(qp)),
                              jnp.where(lane >= HEAD_DIM, qp, jnp.zeros_like(qp))], axis=0)
        s = lax.dot_general(q2, kp, (((1,), (1,)), ((), ())), preferred_element_type=F32)
        s = s + bias_ref[pat, hp]
        m = jnp.max(s, axis=-1, keepdims=True)
        p = jnp.exp(s - m)
        l = jnp.sum(p, axis=-1, keepdims=True)
        o = jnp.dot(p.astype(BF16), vp, preferred_element_type=F32) / l
        o_ref[0, :, cols] = jnp.where(lane < HEAD_DIM, o[:nq], o[nq:]).astype(BF16)


def _attention(q, k, v, bias):
    B, L, _ = q.shape
    rows = L // GRID_W
    assert rows % Q_ROWS == 0 and rows >= 3 * Q_ROWS
    nq = Q_ROWS * GRID_W
    seq = pl.BlockSpec((1, L, D_ATT), lambda b, g: (b, 0, 0))
    return pl.pallas_call(
        functools.partial(_attn_kernel, rows=rows),
        grid=(B, rows // Q_ROWS),
        in_specs=[pl.BlockSpec((1, nq, D_ATT), lambda b, g: (b, g, 0)), seq, seq,
                  pl.BlockSpec(bias.shape, lambda b, g: (0, 0, 0, 0), pipeline_mode=pl.Buffered(1))],
        out_specs=pl.BlockSpec((1, nq, D_ATT), lambda b, g: (b, g, 0)),
        out_shape=jax.ShapeDtypeStruct((B, L, D_ATT), BF16),
        compiler_params=pltpu.CompilerParams(
            dimension_semantics=("arbitrary", "arbitrary"), vmem_limit_bytes=VMEM_LIMIT),
        name="attn",
    )(q, k, v, bias)


def _cmul(ar, ai, br, bi):
    return ar * br - ai * bi, ar * bi + ai * br


def _ssm_tables(lam_re, lam_im, log_dt, b_re, b_im, c_re, c_im):
    hp = lax.Precision.HIGHEST
    lam_re, lam_im = lam_re.astype(F32), lam_im.astype(F32)
    dt = jnp.exp(log_dt.astype(F32))[..., None]
    zr, zi = lam_re * dt, lam_im * dt
    d = jnp.arange(CHUNK + 1, dtype=F32)[:, None, None, None]
    mag = jnp.exp(zr[None] * d)
    pw_re, pw_im = mag * jnp.cos(zi[None] * d), mag * jnp.sin(zi[None] * d)
    nr, ni = pw_re[1] - 1.0, pw_im[1]
    den = lam_re * lam_re + lam_im * lam_im
    fr, fi = (nr * lam_re + ni * lam_im) / den, (ni * lam_re - nr * lam_im) / den
    bb_re, bb_im = _cmul(fr[..., None], fi[..., None], b_re.astype(F32), b_im.astype(F32))
    c_re, c_im = c_re.astype(F32), c_im.astype(F32)

    cp_re, cp_im = _cmul(c_re[None], c_im[None], pw_re[:CHUNK, :, :, None, :], pw_im[:CHUNK, :, :, None, :])
    cp = jnp.transpose(jnp.concatenate([cp_re, -cp_im], axis=-1), (1, 2, 4, 0, 3))
    kern = jnp.einsum('zgqk,zgqn->zgkn', jnp.concatenate([bb_re, bb_im], axis=2),
                      cp.reshape(2, N_GROUPS, 2 * STATE_P, CHUNK * SSM_GROUP), precision=hp)
    kern = jnp.transpose(kern.reshape(2, N_GROUPS, SSM_GROUP, CHUNK, SSM_GROUP), (3, 0, 1, 4, 2))
    kf, kb = kern[:, 0], kern[:, 1]
    k_all = jnp.concatenate([kb[:0:-1], (kf[0] + kb[0])[None], kf[1:]], axis=0)
    m = jnp.stack([k_all[CHUNK - 1 - s:2 * CHUNK - 1 - s] for s in range(CHUNK)])
    m = jnp.transpose(m, (2, 0, 4, 1, 3)).reshape(N_GROUPS, FLAT, FLAT)

    def state_in(z, powers):
        pr, pi = _cmul(pw_re[powers, z][:, :, :, None], pw_im[powers, z][:, :, :, None],
                       bb_re[z][None], bb_im[z][None])
        both = jnp.concatenate([pr, pi], axis=2)
        return jnp.transpose(both, (1, 0, 3, 2)).reshape(N_GROUPS, FLAT, 2 * STATE_P)

    def state_out(z, powers):
        qr, qi = _cmul(c_re[z][None], c_im[z][None],
                       pw_re[powers, z][:, :, None, :], pw_im[powers, z][:, :, None, :])
        both = jnp.concatenate([qr, -qi], axis=3)
        return jnp.transpose(both, (1, 3, 0, 2)).reshape(N_GROUPS, 2 * STATE_P, FLAT)

    fwd = np.arange(CHUNK)
    pp = jnp.concatenate([state_in(0, CHUNK - 1 - fwd), state_in(1, fwd)], axis=2)
    mq = jnp.concatenate([m, state_out(0, fwd + 1), state_out(1, CHUNK - fwd)], axis=1)
    ar, ai = pw_re[CHUNK], pw_im[CHUNK]
    ca = jnp.concatenate([ar, ar], axis=-1)
    cb = jnp.concatenate([-ai, ai], axis=-1)
    coef = jnp.stack([ca[0], cb[0], ca[1], cb[1]])
    return pp.astype(BF16), mq.astype(BF16), coef


def _ssm_kernel(uf_ref, pp_ref, mq_ref, coef_ref, y_ref, s_ref, *, nc):
    half = 2 * STATE_P
    for g in range(N_GROUPS):
        ds = jnp.dot(uf_ref[0, g], pp_ref[g], preferred_element_type=F32)
        s_ref[0, pl.ds(g, nc, stride=N_GROUPS), :] = ds[:, :half]
        s_ref[1, pl.ds(g, nc, stride=N_GROUPS), :] = ds[:, half:]

    caf, cbf, cab, cbb = coef_ref[0], coef_ref[1], coef_ref[2], coef_ref[3]

    def step(c, carry):
        sf, sfx, sb, sbx = carry
        rf = pl.multiple_of(c * N_GROUPS, N_GROUPS)
        rb = pl.multiple_of((nc - 1 - c) * N_GROUPS, N_GROUPS)
        df = s_ref[0, pl.ds(rf, N_GROUPS), :]
        db = s_ref[1, pl.ds(rb, N_GROUPS), :]
        s_ref[0, pl.ds(rf, N_GROUPS), :] = sf
        s_ref[1, pl.ds(rb, N_GROUPS), :] = sb
        dfx = pltpu.roll(df, STATE_P, 1)
        dbx = pltpu.roll(db, STATE_P, 1)
        sf, sfx = caf * sf + cbf * sfx + df, caf * sfx - cbf * sf + dfx
        sb, sbx = cab * sb + cbb * sbx + db, cab * sbx - cbb * sb + dbx
        return sf, sfx, sb, sbx

    zero = jnp.zeros((N_GROUPS, half), F32)
    lax.fori_loop(0, nc, step, (zero, zero, zero, zero), unroll=8)

    for g in range(N_GROUPS):
        sf = s_ref[0, pl.ds(g, nc, stride=N_GROUPS), :].astype(BF16)
        sb = s_ref[1, pl.ds(g, nc, stride=N_GROUPS), :].astype(BF16)
        lhs = jnp.concatenate([uf_ref[0, g], sf, sb], axis=1)
        y_ref[0, g] = jnp.dot(lhs, mq_ref[g], preferred_element_type=F32)


def _ssm(uf, pp, mq, coef):
    B, _, nc, _ = uf.shape
    const = lambda a: pl.BlockSpec(a.shape, lambda b: (0,) * a.ndim, pipeline_mode=pl.Buffered(1))
    blk = pl.BlockSpec((1, N_GROUPS, nc, FLAT), lambda b: (b, 0, 0, 0))
    return pl.pallas_call(
        functools.partial(_ssm_kernel, nc=nc),
        grid=(B,),
        in_specs=[blk, const(pp), const(mq), const(coef)],
        out_specs=blk,
        out_shape=jax.ShapeDtypeStruct((B, N_GROUPS, nc, FLAT), F32),
        scratch_shapes=[pltpu.VMEM((2, nc * N_GROUPS, 2 * STATE_P), F32)],
        compiler_params=pltpu.CompilerParams(
            dimension_semantics=("arbitrary",), vmem_limit_bytes=VMEM_LIMIT),
        name="ssm",
    )(uf, pp, mq, coef)


MERGE_SUB = 256


def _merge_kernel(x_ref, att_ref, yf_ref, u_ref, ga_ref, gs_ref, d_ref, wglu_ref, bglu_ref,
                  watt_ref, wssm_ref, wout_ref, gpost_ref, o_ref, y_sc, *, tm):
    for rb in range(tm // (8 * CHUNK)):
        for jt in range(D_SSM // LANES):
            for half in range(2):
                vs = [yf_ref[0, 8 * jt + gl, rb * 8:(rb + 1) * 8, half * LANES:(half + 1) * LANES]
                      for gl in range(8)]
                out = _block_transpose8(vs)
                for sl in range(8):
                    y_sc[jt, pl.ds(rb * 8 * CHUNK + 8 * half + sl, 8, stride=CHUNK), :] = out[sl]

    for r0 in range(0, tm, MERGE_SUB):
        rows = slice(r0, r0 + MERGE_SUB)
        y = jnp.concatenate([y_sc[jt, rows, :] for jt in range(D_SSM // LANES)], axis=1)
        y = jax.nn.gelu(y + d_ref[...] * u_ref[0, rows, :])
        z = jnp.dot(y.astype(BF16), wglu_ref[...], preferred_element_type=F32) + bglu_ref[...]
        ssm = (y * jax.nn.sigmoid(z)).astype(BF16)
        a = jnp.dot(att_ref[0, rows, :], watt_ref[...], preferred_element_type=F32)
        b = jnp.dot(ssm, wssm_ref[...], preferred_element_type=F32)
        merged = jax.nn.sigmoid(ga_ref[0, rows, :]) * a + jax.nn.sigmoid(gs_ref[0, rows, :]) * b
        mo = jnp.dot(merged.astype(BF16), wout_ref[...], preferred_element_type=F32)
        o_ref[0, rows, :] = x_ref[0, rows, :] + _rms(mo, gpost_ref[...])


def _merge(x, att, yf, u, ga, gs, d_skip, w_glu, b_glu, w_att, w_ssm, w_out, g_post, *, tm=512):
    B, L, _ = x.shape
    row = lambda n: pl.BlockSpec((1, tm, n), lambda b, i: (b, i, 0))
    const = lambda a: pl.BlockSpec(a.shape, lambda b, i: (0,) * a.ndim, pipeline_mode=pl.Buffered(1))
    consts = (d_skip, w_glu, b_glu, w_att, w_ssm, w_out, g_post)
    return pl.pallas_call(
        functools.partial(_merge_kernel, tm=tm),
        grid=(B, L // tm),
        in_specs=[row(D_MODEL), row(D_ATT),
                  pl.BlockSpec((1, N_GROUPS, tm // CHUNK, FLAT), lambda b, i: (b, 0, i, 0)),
                  row(D_SSM), row(D_MODEL), row(D_MODEL)] + [const(a) for a in consts],
        out_specs=row(D_MODEL),
        out_shape=jax.ShapeDtypeStruct((B, L, D_MODEL), F32),
        scratch_shapes=[pltpu.VMEM((D_SSM // LANES, tm, LANES), F32)],
        compiler_params=pltpu.CompilerParams(
            dimension_semantics=("arbitrary", "arbitrary"), vmem_limit_bytes=VMEM_LIMIT),
        name="merge",
    )(x, att, yf, u, ga, gs, *consts)


HALO = 8
FF_CHUNK = 256


def _ffn_kernel(xp_ref, x_ref, xn_ref, gpre_ref, wup_ref, cw_ref, cb_ref, wdn_ref, gpost_ref, o_ref,
                *, tm, n_tiles):
    i = pl.program_id(1)
    x = x_ref[0]
    xp = jnp.where(i == 0, 0.0, xp_ref[0])
    xn = jnp.where(i == n_tiles - 1, 0.0, xn_ref[0])
    h = _rms(jnp.concatenate([xp, x, xn], axis=0), gpre_ref[...]).astype(BF16)
    n_ext = tm + 2 * HALO

    def conv(c0):
        up = jnp.dot(h, wup_ref[:, c0:c0 + FF_CHUNK], preferred_element_type=F32)
        w = cw_ref[:, c0:c0 + FF_CHUNK]
        prev = pltpu.roll(up, 1, 0)
        nxt = pltpu.roll(up, n_ext - 1, 0)
        r = prev * w[0:1] + up * w[1:2] + nxt * w[2:3] + cb_ref[:, c0:c0 + FF_CHUNK]
        return r[HALO:HALO + tm]

    acts = [(jax.nn.gelu(conv(c * FF_CHUNK)) * conv(D_FF + c * FF_CHUNK)).astype(BF16)
            for c in range(D_FF // FF_CHUNK)]
    acc = jnp.dot(jnp.concatenate(acts, axis=1), wdn_ref[...], preferred_element_type=F32)
    o_ref[0] = x + _rms(acc, gpost_ref[...])


def _ffn(x, g_pre, w_up, conv_w, conv_b, w_down, g_post, *, tm=1024):
    B, L, _ = x.shape
    n_tiles = L // tm
    hb = tm // HALO
    const = lambda a: pl.BlockSpec(a.shape, lambda b, i: (0,) * a.ndim, pipeline_mode=pl.Buffered(1))
    consts_a = (g_pre, w_up, conv_w, conv_b, w_down, g_post)
    return pl.pallas_call(
        functools.partial(_ffn_kernel, tm=tm, n_tiles=n_tiles),
        grid=(B, n_tiles),
        in_specs=[pl.BlockSpec((1, HALO, D_MODEL), lambda b, i: (b, jnp.maximum(i * hb - 1, 0), 0)),
                  pl.BlockSpec((1, tm, D_MODEL), lambda b, i: (b, i, 0)),
                  pl.BlockSpec((1, HALO, D_MODEL),
                               lambda b, i: (b, jnp.minimum((i + 1) * hb, L // HALO - 1), 0))]
        + [const(a) for a in consts_a],
        out_specs=pl.BlockSpec((1, tm, D_MODEL), lambda b, i: (b, i, 0)),
        out_shape=jax.ShapeDtypeStruct((B, L, D_MODEL), F32),
        compiler_params=pltpu.CompilerParams(
            dimension_semantics=("arbitrary", "arbitrary"), vmem_limit_bytes=VMEM_LIMIT),
        name="ffn",
    )(x, x, x, *consts_a)


def _layer(x, p):
    q, k, v, u, uf, ga, gs = _inproj(x, p['g_mix_pre'], p['w_in'])
    att = _attention(q, k, v, p['attn_bias'])
    yf = _ssm(uf, p['pp'], p['mq'], p['coef'])
    x1 = _merge(x, att, yf, u, ga, gs, p['ssm_d'], p['w_glu'], p['b_glu'], p['w_branch_att'],
                p['w_branch_ssm'], p['w_out'], p['g_mix_post'])
    return _ffn(x1, p['g_ffn_pre'], p['w_up'], p['conv_w'], p['conv_b'], p['w_down'], p['g_ffn_post'])


def _prepare(l, g_mix_pre, g_mix_post, w_in, attn_rpb, ssm_lam_re, ssm_lam_im, ssm_log_dt, ssm_b_re,
             ssm_b_im, ssm_c_re, ssm_c_im, ssm_d, w_glu, b_glu, w_branch_att, w_branch_ssm, w_out,
             g_ffn_pre, g_ffn_post, w_up, conv_w, conv_b, w_down):
    pp, mq, coef = _ssm_tables(ssm_lam_re[l], ssm_lam_im[l], ssm_log_dt[l], ssm_b_re[l], ssm_b_im[l],
                               ssm_c_re[l], ssm_c_im[l])
    vec = lambda a: a[l].astype(F32).reshape(1, -1)
    return dict(
        g_mix_pre=vec(g_mix_pre), g_mix_post=vec(g_mix_post), w_in=w_in[l].astype(BF16),
        attn_bias=_attn_bias(attn_rpb[l].astype(F32)), pp=pp, mq=mq, coef=coef, ssm_d=vec(ssm_d),
        w_glu=w_glu[l].astype(BF16), b_glu=vec(b_glu), w_branch_att=w_branch_att[l].astype(BF16),
        w_branch_ssm=w_branch_ssm[l].astype(BF16), w_out=w_out[l].astype(BF16),
        g_ffn_pre=vec(g_ffn_pre), g_ffn_post=vec(g_ffn_post), w_up=w_up[l].astype(BF16),
        conv_w=conv_w[l].astype(F32), conv_b=vec(conv_b), w_down=w_down[l].astype(BF16))


def kernel(x_prompt, x_sample, g_mix_pre, g_mix_post, w_in, attn_rpb, ssm_lam_re, ssm_lam_im, ssm_log_dt,
           ssm_b_re, ssm_b_im, ssm_c_re, ssm_c_im, ssm_d, w_glu, b_glu, w_branch_att, w_branch_ssm, w_out,
           g_ffn_pre, g_ffn_post, w_up, conv_w, conv_b, w_down):
    weights = (g_mix_pre, g_mix_post, w_in, attn_rpb, ssm_lam_re, ssm_lam_im, ssm_log_dt, ssm_b_re,
               ssm_b_im, ssm_c_re, ssm_c_im, ssm_d, w_glu, b_glu, w_branch_att, w_branch_ssm, w_out,
               g_ffn_pre, g_ffn_post, w_up, conv_w, conv_b, w_down)
    layers = [_prepare(l, *weights) for l in range(w_in.shape[0])]

    def trunk(x):
        for p in layers:
            x = _layer(x, p)
        return x

    return trunk(x_prompt), trunk(x_sample)
```

```python
import functools

import numpy as np
import jax
import jax.numpy as jnp
from jax import lax
from jax.experimental import pallas as pl
from jax.experimental.pallas import tpu as pltpu

D_MODEL = 1024
GRID_W = 64
N_HEADS = 8
HEAD_DIM = 64
D_ATT = N_HEADS * HEAD_DIM
NA_KH = 8
NA_KW = 16
SSM_GROUP = 16
D_SSM = 512
N_GROUPS = D_SSM // SSM_GROUP
STATE_P = 64
D_FF = 2816
D_IN = 3 * D_ATT + D_SSM + 2 * D_MODEL
EPS = 1e-6
NEG_BIG = -1e30

CHUNK = 16
FLAT = CHUNK * SSM_GROUP
Q_ROWS = 4
K_ROWS = 12
GROUPS_PER_STEP = 4
LANES = 128
VMEM_LIMIT = 56 * 1024 * 1024

F32 = jnp.float32
BF16 = jnp.bfloat16


def _rms(x, g):
    return x * lax.rsqrt(jnp.mean(x * x, axis=-1, keepdims=True) + EPS) * g


def _block_transpose8(vs):
    lane = lax.broadcasted_iota(jnp.int32, vs[0].shape, 1)
    vs = list(vs)
    for d in (4, 2, 1):
        keep = (lane & (d * SSM_GROUP)) == 0
        for r in range(8):
            if r & d:
                continue
            lo, hi = vs[r], vs[r + d]
            vs[r] = jnp.where(keep, lo, pltpu.roll(hi, d * SSM_GROUP, 1))
            vs[r + d] = jnp.where(keep, pltpu.roll(lo, LANES - d * SSM_GROUP, 1), hi)
    return vs


def _inproj_kernel(x_ref, g_ref, w_ref, q_ref, k_ref, v_ref, u_ref, uf_ref, ga_ref, gs_ref, u_sc, *, tm):
    x = x_ref[0]
    h = _rms(x, g_ref[...]).astype(BF16)

    def proj(c0, n):
        return jnp.dot(h, w_ref[:, c0:c0 + n], preferred_element_type=F32)

    q_ref[0] = (proj(0, D_ATT) * (HEAD_DIM ** -0.5)).astype(BF16)
    k_ref[0] = proj(D_ATT, D_ATT).astype(BF16)
    v_ref[0] = proj(2 * D_ATT, D_ATT).astype(BF16)
    u = proj(3 * D_ATT, D_SSM)
    u_ref[0] = u
    for jt in range(D_SSM // LANES):
        u_sc[jt] = u[:, jt * LANES:(jt + 1) * LANES]
    ga_ref[0] = proj(3 * D_ATT + D_SSM, D_MODEL).astype(BF16)
    gs_ref[0] = proj(3 * D_ATT + D_SSM + D_MODEL, D_MODEL).astype(BF16)

    n_rb = tm // (8 * CHUNK)
    pieces = [[[None, None] for _ in range(n_rb)] for _ in range(N_GROUPS)]
    for rb in range(n_rb):
        for jt in range(D_SSM // LANES):
            for half in range(2):
                vs = [u_sc[jt, pl.ds(rb * 8 * CHUNK + 8 * half + sl, 8, stride=CHUNK), :]
                      for sl in range(8)]
                out = _block_transpose8(vs)
                for gl in range(8):
                    pieces[8 * jt + gl][rb][half] = out[gl]
    for g in range(N_GROUPS):
        rows = [jnp.concatenate(pieces[g][rb], axis=1) for rb in range(n_rb)]
        uf_ref[0, g] = jnp.concatenate(rows, axis=0).astype(BF16)


def _inproj(x, g_pre, w_in, *, tm=512):
    B, L, _ = x.shape
    nc = L // CHUNK
    row = lambda n: pl.BlockSpec((1, tm, n), lambda b, i: (b, i, 0))
    const = lambda shape: pl.BlockSpec(shape, lambda b, i: (0,) * len(shape),
                                       pipeline_mode=pl.Buffered(1))
    return pl.pallas_call(
        functools.partial(_inproj_kernel, tm=tm),
        grid=(B, L // tm),
        in_specs=[row(D_MODEL), const((1, D_MODEL)), const((D_MODEL, D_IN))],
        out_specs=[row(D_ATT), row(D_ATT), row(D_ATT), row(D_SSM),
                   pl.BlockSpec((1, N_GROUPS, tm // CHUNK, FLAT), lambda b, i: (b, 0, i, 0)),
                   row(D_MODEL), row(D_MODEL)],
        out_shape=[jax.ShapeDtypeStruct((B, L, D_ATT), BF16)] * 3
        + [jax.ShapeDtypeStruct((B, L, D_SSM), F32),
           jax.ShapeDtypeStruct((B, N_GROUPS, nc, FLAT), BF16),
           jax.ShapeDtypeStruct((B, L, D_MODEL), BF16),
           jax.ShapeDtypeStruct((B, L, D_MODEL), BF16)],
        scratch_shapes=[pltpu.VMEM((D_SSM // LANES, tm, LANES), F32)],
        compiler_params=pltpu.CompilerParams(
            dimension_semantics=("arbitrary", "arbitrary"), vmem_limit_bytes=VMEM_LIMIT),
        name="inproj",
    )(x, g_pre, w_in)


def _attn_bias(rpb):
    qc = np.arange(GRID_W)[:, None]
    kc = np.arange(GRID_W)[None, :]
    col_start = np.clip(qc - NA_KW // 2, 0, GRID_W - NA_KW)
    col_ok = (kc >= col_start) & (kc < col_start + NA_KW)
    pad = GRID_W - NA_KW
    rp = jnp.pad(rpb, ((0, 0), (0, 0), (pad, pad)))
    tab = jnp.stack([rp[:, :, GRID_W - 1 - c:2 * GRID_W - 1 - c] for c in range(GRID_W)], axis=2)
    tab = jnp.where(col_ok[None, None], tab, NEG_BIG)
    neg = jnp.full((N_HEADS, GRID_W, GRID_W), NEG_BIG, F32)
    out = []
    for off, rs_rel in ((0, (0, 0, 0, 0)), (4, (0, 1, 2, 3)), (8, (4, 4, 4, 4))):
        rows = []
        for i in range(Q_ROWS):
            blocks = [tab[:, j - i - off + NA_KH - 1] if rs_rel[i] <= j < rs_rel[i] + NA_KH else neg
                      for j in range(K_ROWS)]
            rows.append(jnp.concatenate(blocks, axis=2))
        out.append(jnp.concatenate(rows, axis=1))
    return jnp.stack(out).astype(F32).reshape(3, N_HEADS // 2, 2 * Q_ROWS * GRID_W, K_ROWS * GRID_W)


def _attn_kernel(q_ref, k_ref, v_ref, bias_ref, o_ref, *, rows):
    n_groups = rows // Q_ROWS
    nq, nk = Q_ROWS * GRID_W, K_ROWS * GRID_W
    lane = lax.broadcasted_iota(jnp.int32, (nq, LANES), 1)
    for gi in range(GROUPS_PER_STEP):
        g = pl.program_id(1) * GROUPS_PER_STEP + gi
        kr0 = jnp.clip(g * Q_ROWS - NA_KH // 2, 0, rows - K_ROWS)
        pat = jnp.where(g == 0, 0, jnp.where(g == n_groups - 1, 2, 1))
        start = pl.multiple_of(kr0 * GRID_W, Q_ROWS * GRID_W)
        qrows = slice(gi * nq, (gi + 1) * nq)
        for hp in range(N_HEADS // 2):
            cols = slice(hp * LANES, (hp + 1) * LANES)
            qp = q_ref[0, qrows, cols]
            kp = k_ref[0, pl.ds(start, nk), cols]
            vp = v_ref[0, pl.ds(start, nk), cols]
            q2 = jnp.concatenate([jnp.where(lane < HEAD_DIM, qp, jnp.zeros_like(qp)),
                                  jnp.where(lane >= HEAD_DIM, qp, jnp.zeros_like(qp))], axis=0)
            s = lax.dot_general(q2, kp, (((1,), (1,)), ((), ())), preferred_element_type=F32)
            s = s + bias_ref[pat, hp]
            m = jnp.max(s, axis=-1, keepdims=True)
            p = jnp.exp(s - m)
            l = jnp.sum(p, axis=-1, keepdims=True)
            o = jnp.dot(p.astype(BF16), vp, preferred_element_type=F32) / l
            o_ref[0, qrows, cols] = jnp.where(lane < HEAD_DIM, o[:nq], o[nq:]).astype(BF16)


def _attention(q, k, v, bias):
    B, L, _ = q.shape
    rows = L // GRID_W
    assert rows % (Q_ROWS * GROUPS_PER_STEP) == 0 and rows >= 3 * Q_ROWS
    nq = Q_ROWS * GRID_W * GROUPS_PER_STEP
    seq = pl.BlockSpec((1, L, D_ATT), lambda b, g: (b, 0, 0))
    return pl.pallas_call(
        functools.partial(_attn_kernel, rows=rows),
        grid=(B, rows // (Q_ROWS * GROUPS_PER_STEP)),
        in_specs=[pl.BlockSpec((1, nq, D_ATT), lambda b, g: (b, g, 0)), seq, seq,
                  pl.BlockSpec(bias.shape, lambda b, g: (0, 0, 0, 0), pipeline_mode=pl.Buffered(1))],
        out_specs=pl.BlockSpec((1, nq, D_ATT), lambda b, g: (b, g, 0)),
        out_shape=jax.ShapeDtypeStruct((B, L, D_ATT), BF16),
        compiler_params=pltpu.CompilerParams(
            dimension_semantics=("arbitrary", "arbitrary"), vmem_limit_bytes=VMEM_LIMIT),
        name="attn",
    )(q, k, v, bias)


def _cmul(ar, ai, br, bi):
    return ar * br - ai * bi, ar * bi + ai * br


def _ssm_tables(lam_re, lam_im, log_dt, b_re, b_im, c_re, c_im):
    hp = lax.Precision.HIGHEST
    lam_re, lam_im = lam_re.astype(F32), lam_im.astype(F32)
    dt = jnp.exp(log_dt.astype(F32))[..., None]
    zr, zi = lam_re * dt, lam_im * dt
    d = jnp.arange(CHUNK + 1, dtype=F32)[:, None, None, None]
    mag = jnp.exp(zr[None] * d)
    pw_re, pw_im = mag * jnp.cos(zi[None] * d), mag * jnp.sin(zi[None] * d)
    nr, ni = pw_re[1] - 1.0, pw_im[1]
    den = lam_re * lam_re + lam_im * lam_im
    fr, fi = (nr * lam_re + ni * lam_im) / den, (ni * lam_re - nr * lam_im) / den
    bb_re, bb_im = _cmul(fr[..., None], fi[..., None], b_re.astype(F32), b_im.astype(F32))
    c_re, c_im = c_re.astype(F32), c_im.astype(F32)

    cp_re, cp_im = _cmul(c_re[None], c_im[None], pw_re[:CHUNK, :, :, None, :], pw_im[:CHUNK, :, :, None, :])
    cp = jnp.transpose(jnp.concatenate([cp_re, -cp_im], axis=-1), (1, 2, 4, 0, 3))
    kern = jnp.einsum('zgqk,zgqn->zgkn', jnp.concatenate([bb_re, bb_im], axis=2),
                      cp.reshape(2, N_GROUPS, 2 * STATE_P, CHUNK * SSM_GROUP), precision=hp)
    kern = jnp.transpose(kern.reshape(2, N_GROUPS, SSM_GROUP, CHUNK, SSM_GROUP), (3, 0, 1, 4, 2))
    kf, kb = kern[:, 0], kern[:, 1]
    k_all = jnp.concatenate([kb[:0:-1], (kf[0] + kb[0])[None], kf[1:]], axis=0)
    m = jnp.stack([k_all[CHUNK - 1 - s:2 * CHUNK - 1 - s] for s in range(CHUNK)])
    m = jnp.transpose(m, (2, 0, 4, 1, 3)).reshape(N_GROUPS, FLAT, FLAT)

    def state_in(z, powers):
        pr, pi = _cmul(pw_re[powers, z][:, :, :, None], pw_im[powers, z][:, :, :, None],
                       bb_re[z][None], bb_im[z][None])
        both = jnp.concatenate([pr, pi], axis=2)
        return jnp.transpose(both, (1, 0, 3, 2)).reshape(N_GROUPS, FLAT, 2 * STATE_P)

    def state_out(z, powers):
        qr, qi = _cmul(c_re[z][None], c_im[z][None],
                       pw_re[powers, z][:, :, None, :], pw_im[powers, z][:, :, None, :])
        both = jnp.concatenate([qr, -qi], axis=3)
        return jnp.transpose(both, (1, 3, 0, 2)).reshape(N_GROUPS, 2 * STATE_P, FLAT)

    fwd = np.arange(CHUNK)
    pp = jnp.concatenate([state_in(0, CHUNK - 1 - fwd), state_in(1, fwd)], axis=2)
    mq = jnp.concatenate([m, state_out(0, fwd + 1), state_out(1, CHUNK - fwd)], axis=1)
    ar, ai = pw_re[CHUNK], pw_im[CHUNK]
    ca = jnp.concatenate([ar, ar], axis=-1)
    cb = jnp.concatenate([-ai, ai], axis=-1)
    coef = jnp.stack([ca[0], cb[0], ca[1], cb[1]])
    return pp.astype(BF16), mq.astype(BF16), coef


def _ssm_kernel(uf_ref, pp_ref, mq_ref, coef_ref, y_ref, s_ref, so_ref, *, nc):
    half = 2 * STATE_P
    for g in range(N_GROUPS):
        ds = jnp.dot(uf_ref[0, g], pp_ref[g], preferred_element_type=F32)
        s_ref[0, pl.ds(g, nc, stride=N_GROUPS), :] = ds[:, :half]
        s_ref[1, pl.ds(g, nc, stride=N_GROUPS), :] = ds[:, half:]

    caf, cbf, cab, cbb = coef_ref[0], coef_ref[1], coef_ref[2], coef_ref[3]

    def step(c, carry):
        sf, sfx, sb, sbx = carry
        rf = pl.multiple_of(c * N_GROUPS, N_GROUPS)
        rb = pl.multiple_of((nc - 1 - c) * N_GROUPS, N_GROUPS)
        df = s_ref[0, pl.ds(rf, N_GROUPS), :]
        db = s_ref[1, pl.ds(rb, N_GROUPS), :]
        so_ref[0, pl.ds(rf, N_GROUPS), :] = sf
        so_ref[1, pl.ds(rb, N_GROUPS), :] = sb
        dfx = pltpu.roll(df, STATE_P, 1)
        dbx = pltpu.roll(db, STATE_P, 1)
        sf, sfx = caf * sf + cbf * sfx + df, caf * sfx - cbf * sf + dfx
        sb, sbx = cab * sb + cbb * sbx + db, cab * sbx - cbb * sb + dbx
        return sf, sfx, sb, sbx

    zero = jnp.zeros((N_GROUPS, half), F32)
    lax.fori_loop(0, nc, step, (zero, zero, zero, zero), unroll=8)

    for g in range(N_GROUPS):
        sf = so_ref[0, pl.ds(g, nc, stride=N_GROUPS), :].astype(BF16)
        sb = so_ref[1, pl.ds(g, nc, stride=N_GROUPS), :].astype(BF16)
        lhs = jnp.concatenate([uf_ref[0, g], sf, sb], axis=1)
        y_ref[0, g] = jnp.dot(lhs, mq_ref[g], preferred_element_type=F32)


def _ssm(uf, pp, mq, coef):
    B, _, nc, _ = uf.shape
    const = lambda a: pl.BlockSpec(a.shape, lambda b: (0,) * a.ndim, pipeline_mode=pl.Buffered(1))
    blk = pl.BlockSpec((1, N_GROUPS, nc, FLAT), lambda b: (b, 0, 0, 0))
    return pl.pallas_call(
        functools.partial(_ssm_kernel, nc=nc),
        grid=(B,),
        in_specs=[blk, const(pp), const(mq), const(coef)],
        out_specs=blk,
        out_shape=jax.ShapeDtypeStruct((B, N_GROUPS, nc, FLAT), F32),
        scratch_shapes=[pltpu.VMEM((2, nc * N_GROUPS, 2 * STATE_P), F32)] * 2,
        compiler_params=pltpu.CompilerParams(
            dimension_semantics=("arbitrary",), vmem_limit_bytes=VMEM_LIMIT),
        name="ssm",
    )(uf, pp, mq, coef)


MERGE_SUB = 256


def _merge_kernel(x_ref, att_ref, yf_ref, u_ref, ga_ref, gs_ref, d_ref, wglu_ref, bglu_ref,
                  watt_ref, wssm_ref, wout_ref, gpost_ref, o_ref, y_sc, *, tm):
    for rb in range(tm // (8 * CHUNK)):
        for jt in range(D_SSM // LANES):
            for half in range(2):
                vs = [yf_ref[0, 8 * jt + gl, rb * 8:(rb + 1) * 8, half * LANES:(half + 1) * LANES]
                      for gl in range(8)]
                out = _block_transpose8(vs)
                for sl in range(8):
                    y_sc[jt, pl.ds(rb * 8 * CHUNK + 8 * half + sl, 8, stride=CHUNK), :] = out[sl]

    for r0 in range(0, tm, MERGE_SUB):
        rows = slice(r0, r0 + MERGE_SUB)
        y = jnp.concatenate([y_sc[jt, rows, :] for jt in range(D_SSM // LANES)], axis=1)
        y = jax.nn.gelu(y + d_ref[...] * u_ref[0, rows, :])
        z = jnp.dot(y.astype(BF16), wglu_ref[...], preferred_element_type=F32) + bglu_ref[...]
        ssm = (y * jax.nn.sigmoid(z)).astype(BF16)
        a = jnp.dot(att_ref[0, rows, :], watt_ref[...], preferred_element_type=F32)
        b = jnp.dot(ssm, wssm_ref[...], preferred_element_type=F32)
        merged = (jax.nn.sigmoid(ga_ref[0, rows, :].astype(F32)) * a
                  + jax.nn.sigmoid(gs_ref[0, rows, :].astype(F32)) * b)
        mo = jnp.dot(merged.astype(BF16), wout_ref[...], preferred_element_type=F32)
        o_ref[0, rows, :] = x_ref[0, rows, :] + _rms(mo, gpost_ref[...])


def _merge(x, att, yf, u, ga, gs, d_skip, w_glu, b_glu, w_att, w_ssm, w_out, g_post, *, tm=1024):
    B, L, _ = x.shape
    row = lambda n: pl.BlockSpec((1, tm, n), lambda b, i: (b, i, 0))
    const = lambda a: pl.BlockSpec(a.shape, lambda b, i: (0,) * a.ndim, pipeline_mode=pl.Buffered(1))
    consts = (d_skip, w_glu, b_glu, w_att, w_ssm, w_out, g_post)
    return pl.pallas_call(
        functools.partial(_merge_kernel, tm=tm),
        grid=(B, L // tm),
        in_specs=[row(D_MODEL), row(D_ATT),
                  pl.BlockSpec((1, N_GROUPS, tm // CHUNK, FLAT), lambda b, i: (b, 0, i, 0)),
                  row(D_SSM), row(D_MODEL), row(D_MODEL)] + [const(a) for a in consts],
        out_specs=row(D_MODEL),
        out_shape=jax.ShapeDtypeStruct((B, L, D_MODEL), F32),
        scratch_shapes=[pltpu.VMEM((D_SSM // LANES, tm, LANES), F32)],
        compiler_params=pltpu.CompilerParams(
            dimension_semantics=("arbitrary", "arbitrary"), vmem_limit_bytes=VMEM_LIMIT),
        name="merge",
    )(x, att, yf, u, ga, gs, *consts)


HALO = 8
FF_CHUNK = 256


def _ffn_kernel(xp_ref, x_ref, xn_ref, gpre_ref, wup_ref, cw_ref, cb_ref, wdn_ref, gpost_ref, o_ref,
                *, tm, n_tiles):
    i = pl.program_id(1)
    x = x_ref[0]
    xp = jnp.where(i == 0, 0.0, xp_ref[0])
    xn = jnp.where(i == n_tiles - 1, 0.0, xn_ref[0])
    h = _rms(jnp.concatenate([xp, x, xn], axis=0), gpre_ref[...]).astype(BF16)
    n_ext = tm + 2 * HALO

    def conv(c0):
        up = jnp.dot(h, wup_ref[:, c0:c0 + FF_CHUNK], preferred_element_type=F32)
        w = cw_ref[:, c0:c0 + FF_CHUNK]
        prev = pltpu.roll(up, 1, 0)
        nxt = pltpu.roll(up, n_ext - 1, 0)
        r = prev * w[0:1] + up * w[1:2] + nxt * w[2:3] + cb_ref[:, c0:c0 + FF_CHUNK]
        return r[HALO:HALO + tm]

    acts = [(jax.nn.gelu(conv(c * FF_CHUNK)) * conv(D_FF + c * FF_CHUNK)).astype(BF16)
            for c in range(D_FF // FF_CHUNK)]
    acc = jnp.dot(jnp.concatenate(acts, axis=1), wdn_ref[...], preferred_element_type=F32)
    o_ref[0] = x + _rms(acc, gpost_ref[...])


def _ffn(x, g_pre, w_up, conv_w, conv_b, w_down, g_post, *, tm=1024):
    B, L, _ = x.shape
    n_tiles = L // tm
    hb = tm // HALO
    const = lambda a: pl.BlockSpec(a.shape, lambda b, i: (0,) * a.ndim, pipeline_mode=pl.Buffered(1))
    consts_a = (g_pre, w_up, conv_w, conv_b, w_down, g_post)
    return pl.pallas_call(
        functools.partial(_ffn_kernel, tm=tm, n_tiles=n_tiles),
        grid=(B, n_tiles),
        in_specs=[pl.BlockSpec((1, HALO, D_MODEL), lambda b, i: (b, jnp.maximum(i * hb - 1, 0), 0)),
                  pl.BlockSpec((1, tm, D_MODEL), lambda b, i: (b, i, 0)),
                  pl.BlockSpec((1, HALO, D_MODEL),
                               lambda b, i: (b, jnp.minimum((i + 1) * hb, L // HALO - 1), 0))]
        + [const(a) for a in consts_a],
        out_specs=pl.BlockSpec((1, tm, D_MODEL), lambda b, i: (b, i, 0)),
        out_shape=jax.ShapeDtypeStruct((B, L, D_MODEL), F32),
        compiler_params=pltpu.CompilerParams(
            dimension_semantics=("arbitrary", "arbitrary"), vmem_limit_bytes=VMEM_LIMIT),
        name="ffn",
    )(x, x, x, *consts_a)


def _layer(x, p):
    q, k, v, u, uf, ga, gs = _inproj(x, p['g_mix_pre'], p['w_in'])
    att = _attention(q, k, v, p['attn_bias'])
    yf = _ssm(uf, p['pp'], p['mq'], p['coef'])
    x1 = _merge(x, att, yf, u, ga, gs, p['ssm_d'], p['w_glu'], p['b_glu'], p['w_branch_att'],
                p['w_branch_ssm'], p['w_out'], p['g_mix_post'])
    return _ffn(x1, p['g_ffn_pre'], p['w_up'], p['conv_w'], p['conv_b'], p['w_down'], p['g_ffn_post'])


def _prepare(l, g_mix_pre, g_mix_post, w_in, attn_rpb, ssm_lam_re, ssm_lam_im, ssm_log_dt, ssm_b_re,
             ssm_b_im, ssm_c_re, ssm_c_im, ssm_d, w_glu, b_glu, w_branch_att, w_branch_ssm, w_out,
             g_ffn_pre, g_ffn_post, w_up, conv_w, conv_b, w_down):
    pp, mq, coef = _ssm_tables(ssm_lam_re[l], ssm_lam_im[l], ssm_log_dt[l], ssm_b_re[l], ssm_b_im[l],
                               ssm_c_re[l], ssm_c_im[l])
    vec = lambda a: a[l].astype(F32).reshape(1, -1)
    return dict(
        g_mix_pre=vec(g_mix_pre), g_mix_post=vec(g_mix_post), w_in=w_in[l].astype(BF16),
        attn_bias=_attn_bias(attn_rpb[l].astype(F32)), pp=pp, mq=mq, coef=coef, ssm_d=vec(ssm_d),
        w_glu=w_glu[l].astype(BF16), b_glu=vec(b_glu), w_branch_att=w_branch_att[l].astype(BF16),
        w_branch_ssm=w_branch_ssm[l].astype(BF16), w_out=w_out[l].astype(BF16),
        g_ffn_pre=vec(g_ffn_pre), g_ffn_post=vec(g_ffn_post), w_up=w_up[l].astype(BF16),
        conv_w=conv_w[l].astype(F32), conv_b=vec(conv_b), w_down=w_down[l].astype(BF16))


def kernel(x_prompt, x_sample, g_mix_pre, g_mix_post, w_in, attn_rpb, ssm_lam_re, ssm_lam_im, ssm_log_dt,
           ssm_b_re, ssm_b_im, ssm_c_re, ssm_c_im, ssm_d, w_glu, b_glu, w_branch_att, w_branch_ssm, w_out,
           g_ffn_pre, g_ffn_post, w_up, conv_w, conv_b, w_down):
    weights = (g_mix_pre, g_mix_post, w_in, attn_rpb, ssm_lam_re, ssm_lam_im, ssm_log_dt, ssm_b_re,
               ssm_b_im, ssm_c_re, ssm_c_im, ssm_d, w_glu, b_glu, w_branch_att, w_branch_ssm, w_out,
               g_ffn_pre, g_ffn_post, w_up, conv_w, conv_b, w_down)
    layers = [_prepare(l, *weights) for l in range(w_in.shape[0])]

    def trunk(x):
        for p in layers:
            x = _layer(x, p)
        return x

    return trunk(x_prompt), trunk(x_sample)
```

```python
import functools

import numpy as np
import jax
import jax.numpy as jnp
from jax import lax
from jax.experimental import pallas as pl
from jax.experimental.pallas import tpu as pltpu

D_MODEL = 1024
GRID_W = 64
N_HEADS = 8
HEAD_DIM = 64
D_ATT = N_HEADS * HEAD_DIM
NA_KH = 8
NA_KW = 16
SSM_GROUP = 16
D_SSM = 512
N_GROUPS = D_SSM // SSM_GROUP
STATE_P = 64
D_FF = 2816
D_IN = 3 * D_ATT + D_SSM + 2 * D_MODEL
EPS = 1e-6
NEG_BIG = -1e30

CHUNK = 16
FLAT = CHUNK * SSM_GROUP
Q_ROWS = 4
K_ROWS = 12
GROUPS_PER_STEP = 4
LANES = 128
VMEM_LIMIT = 56 * 1024 * 1024
ATTN_VMEM_LIMIT = 62 * 1024 * 1024

F32 = jnp.float32
BF16 = jnp.bfloat16


def _rms(x, g):
    return x * lax.rsqrt(jnp.mean(x * x, axis=-1, keepdims=True) + EPS) * g


def _block_transpose8(vs):
    lane = lax.broadcasted_iota(jnp.int32, vs[0].shape, 1)
    vs = list(vs)
    for d in (4, 2, 1):
        keep = (lane & (d * SSM_GROUP)) == 0
        for r in range(8):
            if r & d:
                continue
            lo, hi = vs[r], vs[r + d]
            vs[r] = jnp.where(keep, lo, pltpu.roll(hi, d * SSM_GROUP, 1))
            vs[r + d] = jnp.where(keep, pltpu.roll(lo, LANES - d * SSM_GROUP, 1), hi)
    return vs


def _sublane_transpose8(vs):
    sub = lax.broadcasted_iota(jnp.int32, vs[0].shape, 0)
    vs = list(vs)
    for d in (4, 2, 1):
        keep = (sub & d) == 0
        for r in range(8):
            if r & d:
                continue
            lo, hi = vs[r], vs[r + d]
            vs[r] = jnp.where(keep, lo, pltpu.roll(hi, d, 0))
            vs[r + d] = jnp.where(keep, pltpu.roll(lo, 8 - d, 0), hi)
    return vs


def _inproj_kernel(x_ref, g_ref, w_ref, q_ref, k_ref, v_ref, u_ref, uf_ref, ga_ref, gs_ref, u_sc, *, tm):
    x = x_ref[0]
    h = _rms(x, g_ref[...]).astype(BF16)

    def proj(c0, n):
        return jnp.dot(h, w_ref[:, c0:c0 + n], preferred_element_type=F32)

    q_ref[0] = (proj(0, D_ATT) * (HEAD_DIM ** -0.5)).astype(BF16)
    k_ref[0] = proj(D_ATT, D_ATT).astype(BF16)
    v_ref[0] = proj(2 * D_ATT, D_ATT).astype(BF16)
    u = proj(3 * D_ATT, D_SSM)
    u_ref[0] = u
    for jt in range(D_SSM // LANES):
        u_sc[jt] = u[:, jt * LANES:(jt + 1) * LANES]
    ga_ref[0] = proj(3 * D_ATT + D_SSM, D_MODEL).astype(BF16)
    gs_ref[0] = proj(3 * D_ATT + D_SSM + D_MODEL, D_MODEL).astype(BF16)

    n_rb = tm // (8 * CHUNK)
    pieces = [[[None, None] for _ in range(n_rb)] for _ in range(N_GROUPS)]
    for rb in range(n_rb):
        for jt in range(D_SSM // LANES):
            for half in range(2):
                vs = [u_sc[jt, pl.ds(rb * 8 * CHUNK + 8 * half + sl, 8, stride=CHUNK), :]
                      for sl in range(8)]
                out = _block_transpose8(vs)
                for gl in range(8):
                    pieces[8 * jt + gl][rb][half] = out[gl]
    for g in range(N_GROUPS):
        rows = [jnp.concatenate(pieces[g][rb], axis=1) for rb in range(n_rb)]
        uf_ref[0, g] = jnp.concatenate(rows, axis=0).astype(BF16)


def _inproj(x, g_pre, w_in, *, tm=512):
    B, L, _ = x.shape
    nc = L // CHUNK
    row = lambda n: pl.BlockSpec((1, tm, n), lambda b, i: (b, i, 0))
    const = lambda shape: pl.BlockSpec(shape, lambda b, i: (0,) * len(shape),
                                       pipeline_mode=pl.Buffered(1))
    return pl.pallas_call(
        functools.partial(_inproj_kernel, tm=tm),
        grid=(B, L // tm),
        in_specs=[row(D_MODEL), const((1, D_MODEL)), const((D_MODEL, D_IN))],
        out_specs=[row(D_ATT), row(D_ATT), row(D_ATT), row(D_SSM),
                   pl.BlockSpec((1, N_GROUPS, tm // CHUNK, FLAT), lambda b, i: (b, 0, i, 0)),
                   row(D_MODEL), row(D_MODEL)],
        out_shape=[jax.ShapeDtypeStruct((B, L, D_ATT), BF16)] * 3
        + [jax.ShapeDtypeStruct((B, L, D_SSM), F32),
           jax.ShapeDtypeStruct((B, N_GROUPS, nc, FLAT), BF16),
           jax.ShapeDtypeStruct((B, L, D_MODEL), BF16),
           jax.ShapeDtypeStruct((B, L, D_MODEL), BF16)],
        scratch_shapes=[pltpu.VMEM((D_SSM // LANES, tm, LANES), F32)],
        compiler_params=pltpu.CompilerParams(
            dimension_semantics=("arbitrary", "arbitrary"), vmem_limit_bytes=VMEM_LIMIT),
        name="inproj",
    )(x, g_pre, w_in)


N_DR = 2 * NA_KH - 1
_ROW_PATTERNS = ((0, (0, 0, 0, 0)), (4, (0, 1, 2, 3)), (8, (4, 4, 4, 4)))


def _attn_bias_blocks(rpb):
    qc = np.arange(GRID_W)[:, None]
    kc = np.arange(GRID_W)[None, :]
    col_start = np.clip(qc - NA_KW // 2, 0, GRID_W - NA_KW)
    col_ok = (kc >= col_start) & (kc < col_start + NA_KW)
    pad = GRID_W - NA_KW
    rp = jnp.pad(rpb, ((0, 0), (0, 0), (pad, pad)))
    tab = jnp.stack([rp[:, :, GRID_W - 1 - c:2 * GRID_W - 1 - c] for c in range(GRID_W)], axis=2)
    tab = jnp.where(col_ok[None, None], tab, NEG_BIG)
    tab = jnp.concatenate([tab, jnp.full((N_HEADS, 1, GRID_W, GRID_W), NEG_BIG, F32)], axis=1)
    return jnp.concatenate([tab, tab], axis=3).astype(F32)


def _assemble_bias(tab_ref, bias_sc):
    def per_head(h, carry):
        r0 = (h % 2) * (Q_ROWS * GRID_W)
        for p, (off, first_key) in enumerate(_ROW_PATTERNS):
            for i in range(Q_ROWS):
                for j in range(K_ROWS):
                    ok = first_key[i] <= j < first_key[i] + NA_KH
                    dr = j - i - off + NA_KH - 1 if ok else N_DR
                    half = slice((j % 2) * GRID_W, (j % 2 + 1) * GRID_W)
                    bias_sc[p, h // 2, pl.ds(pl.multiple_of(r0 + i * GRID_W, GRID_W), GRID_W),
                            j * GRID_W:(j + 1) * GRID_W] = tab_ref[h, dr, :, half]
        return carry
    lax.fori_loop(0, N_HEADS, per_head, 0)


def _attn_kernel(q_ref, k_ref, v_ref, tab_ref, o_ref, bias_ref, *, rows):
    n_groups = rows // Q_ROWS
    nq, nk = Q_ROWS * GRID_W, K_ROWS * GRID_W
    lane = lax.broadcasted_iota(jnp.int32, (nq, LANES), 1)

    @pl.when((pl.program_id(0) == 0) & (pl.program_id(1) == 0))
    def _():
        _assemble_bias(tab_ref, bias_ref)

    for gi in range(GROUPS_PER_STEP):
        g = pl.program_id(1) * GROUPS_PER_STEP + gi
        kr0 = jnp.clip(g * Q_ROWS - NA_KH // 2, 0, rows - K_ROWS)
        pat = jnp.where(g == 0, 0, jnp.where(g == n_groups - 1, 2, 1))
        start = pl.multiple_of(kr0 * GRID_W, Q_ROWS * GRID_W)
        qrows = slice(gi * nq, (gi + 1) * nq)
        for hp in range(N_HEADS // 2):
            cols = slice(hp * LANES, (hp + 1) * LANES)
            qp = q_ref[0, qrows, cols]
            kp = k_ref[0, pl.ds(start, nk), cols]
            vp = v_ref[0, pl.ds(start, nk), cols]
            q2 = jnp.concatenate([jnp.where(lane < HEAD_DIM, qp, jnp.zeros_like(qp)),
                                  jnp.where(lane >= HEAD_DIM, qp, jnp.zeros_like(qp))], axis=0)
            s = lax.dot_general(q2, kp, (((1,), (1,)), ((), ())), preferred_element_type=F32)
            s = s + bias_ref[pat, hp]
            m = jnp.max(s, axis=-1, keepdims=True)
            p = jnp.exp(s - m)
            l = jnp.sum(p, axis=-1, keepdims=True)
            o = jnp.dot(p.astype(BF16), vp, preferred_element_type=F32) / l
            o_ref[0, qrows, cols] = jnp.where(lane < HEAD_DIM, o[:nq], o[nq:]).astype(BF16)


def _attention(q, k, v, bias_blocks):
    B, L, _ = q.shape
    rows = L // GRID_W
    assert rows % (Q_ROWS * GROUPS_PER_STEP) == 0 and rows >= 3 * Q_ROWS
    nq = Q_ROWS * GRID_W * GROUPS_PER_STEP
    seq = pl.BlockSpec((1, L, D_ATT), lambda b, g: (b, 0, 0))
    return pl.pallas_call(
        functools.partial(_attn_kernel, rows=rows),
        grid=(B, rows // (Q_ROWS * GROUPS_PER_STEP)),
        in_specs=[pl.BlockSpec((1, nq, D_ATT), lambda b, g: (b, g, 0)), seq, seq,
                  pl.BlockSpec(bias_blocks.shape, lambda b, g: (0, 0, 0, 0), pipeline_mode=pl.Buffered(1))],
        out_specs=pl.BlockSpec((1, nq, D_ATT), lambda b, g: (b, g, 0)),
        out_shape=jax.ShapeDtypeStruct((B, L, D_ATT), BF16),
        scratch_shapes=[pltpu.VMEM((len(_ROW_PATTERNS), N_HEADS // 2, 2 * Q_ROWS * GRID_W, K_ROWS * GRID_W), F32)],
        compiler_params=pltpu.CompilerParams(
            dimension_semantics=("arbitrary", "arbitrary"), vmem_limit_bytes=ATTN_VMEM_LIMIT),
        name="attn",
    )(q, k, v, bias_blocks)


def _cmul(ar, ai, br, bi):
    return ar * br - ai * bi, ar * bi + ai * br


def _ssm_tables(lam_re, lam_im, log_dt, b_re, b_im, c_re, c_im):
    hp = lax.Precision.HIGHEST
    lam_re, lam_im = lam_re.astype(F32), lam_im.astype(F32)
    dt = jnp.exp(log_dt.astype(F32))[..., None]
    zr, zi = lam_re * dt, lam_im * dt
    d = jnp.arange(CHUNK + 1, dtype=F32)[:, None, None, None]
    mag = jnp.exp(zr[None] * d)
    pw_re, pw_im = mag * jnp.cos(zi[None] * d), mag * jnp.sin(zi[None] * d)
    nr, ni = pw_re[1] - 1.0, pw_im[1]
    den = lam_re * lam_re + lam_im * lam_im
    fr, fi = (nr * lam_re + ni * lam_im) / den, (ni * lam_re - nr * lam_im) / den
    bb_re, bb_im = _cmul(fr[..., None], fi[..., None], b_re.astype(F32), b_im.astype(F32))
    c_re, c_im = c_re.astype(F32), c_im.astype(F32)

    cp_re, cp_im = _cmul(c_re[None], c_im[None], pw_re[:CHUNK, :, :, None, :], pw_im[:CHUNK, :, :, None, :])
    cp = jnp.transpose(jnp.concatenate([cp_re, -cp_im], axis=-1), (1, 2, 4, 0, 3))
    kern = jnp.einsum('zgqk,zgqn->zgkn', jnp.concatenate([bb_re, bb_im], axis=2),
                      cp.reshape(2, N_GROUPS, 2 * STATE_P, CHUNK * SSM_GROUP), precision=hp)
    kern = jnp.transpose(kern.reshape(2, N_GROUPS, SSM_GROUP, CHUNK, SSM_GROUP), (3, 0, 1, 4, 2))
    kf, kb = kern[:, 0], kern[:, 1]
    k_all = jnp.concatenate([kb[:0:-1], (kf[0] + kb[0])[None], kf[1:]], axis=0)
    m = jnp.stack([k_all[CHUNK - 1 - s:2 * CHUNK - 1 - s] for s in range(CHUNK)])
    m = jnp.transpose(m, (2, 0, 4, 1, 3)).reshape(N_GROUPS, FLAT, FLAT)

    def state_in(z, powers):
        pr, pi = _cmul(pw_re[powers, z][:, :, :, None], pw_im[powers, z][:, :, :, None],
                       bb_re[z][None], bb_im[z][None])
        both = jnp.concatenate([pr, pi], axis=2)
        return jnp.transpose(both, (1, 0, 3, 2)).reshape(N_GROUPS, FLAT, 2 * STATE_P)

    def state_out(z, powers):
        qr, qi = _cmul(c_re[z][None], c_im[z][None],
                       pw_re[powers, z][:, :, None, :], pw_im[powers, z][:, :, None, :])
        both = jnp.concatenate([qr, -qi], axis=3)
        return jnp.transpose(both, (1, 3, 0, 2)).reshape(N_GROUPS, 2 * STATE_P, FLAT)

    fwd = np.arange(CHUNK)
    pp = jnp.concatenate([state_in(0, CHUNK - 1 - fwd), state_in(1, fwd)], axis=2)
    mq = jnp.concatenate([m, state_out(0, fwd + 1), state_out(1, CHUNK - fwd)], axis=1)
    ar, ai = pw_re[CHUNK], pw_im[CHUNK]
    ca = jnp.concatenate([ar, ar], axis=-1)
    cb = jnp.concatenate([-ai, ai], axis=-1)
    coef = jnp.stack([ca[0], cb[0], ca[1], cb[1]])
    return pp.astype(BF16), mq.astype(BF16), coef


def _ssm_kernel(uf_ref, pp_ref, mq_ref, coef_ref, y_ref, s_ref, so_ref, *, nc):
    half = 2 * STATE_P
    for gb in range(N_GROUPS // 8):
        ds = [jnp.dot(uf_ref[0, 8 * gb + gl], pp_ref[8 * gb + gl], preferred_element_type=F32)
              for gl in range(8)]
        for z in range(2):
            for cb in range(nc // 8):
                out = _sublane_transpose8([d[cb * 8:(cb + 1) * 8, z * half:(z + 1) * half] for d in ds])
                for cl in range(8):
                    r0 = (cb * 8 + cl) * N_GROUPS + 8 * gb
                    s_ref[z, r0:r0 + 8, :] = out[cl]

    caf, cbf, cab, cbb = coef_ref[0], coef_ref[1], coef_ref[2], coef_ref[3]

    def step(c, carry):
        sf, sfx, sb, sbx = carry
        rf = pl.multiple_of(c * N_GROUPS, N_GROUPS)
        rb = pl.multiple_of((nc - 1 - c) * N_GROUPS, N_GROUPS)
        df = s_ref[0, pl.ds(rf, N_GROUPS), :]
        db = s_ref[1, pl.ds(rb, N_GROUPS), :]
        so_ref[0, pl.ds(rf, N_GROUPS), :] = sf
        so_ref[1, pl.ds(rb, N_GROUPS), :] = sb
        dfx = pltpu.roll(df, STATE_P, 1)
        dbx = pltpu.roll(db, STATE_P, 1)
        sf, sfx = caf * sf + cbf * sfx + df, caf * sfx - cbf * sf + dfx
        sb, sbx = cab * sb + cbb * sbx + db, cab * sbx - cbb * sb + dbx
        return sf, sfx, sb, sbx

    zero = jnp.zeros((N_GROUPS, half), F32)
    lax.fori_loop(0, nc, step, (zero, zero, zero, zero), unroll=8)

    for gb in range(N_GROUPS // 8):
        states = [[[], []] for _ in range(8)]
        for z in range(2):
            for cb in range(nc // 8):
                r0s = [(cb * 8 + cl) * N_GROUPS + 8 * gb for cl in range(8)]
                out = _sublane_transpose8([so_ref[z, r0:r0 + 8, :] for r0 in r0s])
                for gl in range(8):
                    states[gl][z].append(out[gl])
        for gl in range(8):
            g = 8 * gb + gl
            sf = jnp.concatenate(states[gl][0], axis=0).astype(BF16)
            sb = jnp.concatenate(states[gl][1], axis=0).astype(BF16)
            lhs = jnp.concatenate([uf_ref[0, g], sf, sb], axis=1)
            y_ref[0, g] = jnp.dot(lhs, mq_ref[g], preferred_element_type=F32)


def _ssm(uf, pp, mq, coef):
    B, _, nc, _ = uf.shape
    const = lambda a: pl.BlockSpec(a.shape, lambda b: (0,) * a.ndim, pipeline_mode=pl.Buffered(1))
    blk = pl.BlockSpec((1, N_GROUPS, nc, FLAT), lambda b: (b, 0, 0, 0))
    return pl.pallas_call(
        functools.partial(_ssm_kernel, nc=nc),
        grid=(B,),
        in_specs=[blk, const(pp), const(mq), const(coef)],
        out_specs=blk,
        out_shape=jax.ShapeDtypeStruct((B, N_GROUPS, nc, FLAT), F32),
        scratch_shapes=[pltpu.VMEM((2, nc * N_GROUPS, 2 * STATE_P), F32)] * 2,
        compiler_params=pltpu.CompilerParams(
            dimension_semantics=("arbitrary",), vmem_limit_bytes=VMEM_LIMIT),
        name="ssm",
    )(uf, pp, mq, coef)


MERGE_SUB = 256


def _merge_kernel(x_ref, att_ref, yf_ref, u_ref, ga_ref, gs_ref, d_ref, wglu_ref, bglu_ref,
                  watt_ref, wssm_ref, wout_ref, gpost_ref, o_ref, y_sc, *, tm):
    for rb in range(tm // (8 * CHUNK)):
        for jt in range(D_SSM // LANES):
            for half in range(2):
                vs = [yf_ref[0, 8 * jt + gl, rb * 8:(rb + 1) * 8, half * LANES:(half + 1) * LANES]
                      for gl in range(8)]
                out = _block_transpose8(vs)
                for sl in range(8):
                    y_sc[jt, pl.ds(rb * 8 * CHUNK + 8 * half + sl, 8, stride=CHUNK), :] = out[sl]

    for r0 in range(0, tm, MERGE_SUB):
        rows = slice(r0, r0 + MERGE_SUB)
        y = jnp.concatenate([y_sc[jt, rows, :] for jt in range(D_SSM // LANES)], axis=1)
        y = jax.nn.gelu(y + d_ref[...] * u_ref[0, rows, :])
        z = jnp.dot(y.astype(BF16), wglu_ref[...], preferred_element_type=F32) + bglu_ref[...]
        ssm = (y * jax.nn.sigmoid(z)).astype(BF16)
        a = jnp.dot(att_ref[0, rows, :], watt_ref[...], preferred_element_type=F32)
        b = jnp.dot(ssm, wssm_ref[...], preferred_element_type=F32)
        merged = (jax.nn.sigmoid(ga_ref[0, rows, :].astype(F32)) * a
                  + jax.nn.sigmoid(gs_ref[0, rows, :].astype(F32)) * b)
        mo = jnp.dot(merged.astype(BF16), wout_ref[...], preferred_element_type=F32)
        o_ref[0, rows, :] = x_ref[0, rows, :] + _rms(mo, gpost_ref[...])


def _merge(x, att, yf, u, ga, gs, d_skip, w_glu, b_glu, w_att, w_ssm, w_out, g_post, *, tm=1024):
    B, L, _ = x.shape
    row = lambda n: pl.BlockSpec((1, tm, n), lambda b, i: (b, i, 0))
    const = lambda a: pl.BlockSpec(a.shape, lambda b, i: (0,) * a.ndim, pipeline_mode=pl.Buffered(1))
    consts = (d_skip, w_glu, b_glu, w_att, w_ssm, w_out, g_post)
    return pl.pallas_call(
        functools.partial(_merge_kernel, tm=tm),
        grid=(B, L // tm),
        in_specs=[row(D_MODEL), row(D_ATT),
                  pl.BlockSpec((1, N_GROUPS, tm // CHUNK, FLAT), lambda b, i: (b, 0, i, 0)),
                  row(D_SSM), row(D_MODEL), row(D_MODEL)] + [const(a) for a in consts],
        out_specs=row(D_MODEL),
        out_shape=jax.ShapeDtypeStruct((B, L, D_MODEL), F32),
        scratch_shapes=[pltpu.VMEM((D_SSM // LANES, tm, LANES), F32)],
        compiler_params=pltpu.CompilerParams(
            dimension_semantics=("arbitrary", "arbitrary"), vmem_limit_bytes=VMEM_LIMIT),
        name="merge",
    )(x, att, yf, u, ga, gs, *consts)


HALO = 8
FF_CHUNK = 256


def _ffn_kernel(xp_ref, x_ref, xn_ref, gpre_ref, wup_ref, cw_ref, cb_ref, wdn_ref, gpost_ref, o_ref,
                *, tm, n_tiles):
    i = pl.program_id(1)
    x = x_ref[0]
    xp = jnp.where(i == 0, 0.0, xp_ref[0])
    xn = jnp.where(i == n_tiles - 1, 0.0, xn_ref[0])
    h = _rms(jnp.concatenate([xp, x, xn], axis=0), gpre_ref[...]).astype(BF16)
    n_ext = tm + 2 * HALO

    def conv(c0):
        up = jnp.dot(h, wup_ref[:, c0:c0 + FF_CHUNK], preferred_element_type=F32)
        w = cw_ref[:, c0:c0 + FF_CHUNK]
        prev = pltpu.roll(up, 1, 0)
        nxt = pltpu.roll(up, n_ext - 1, 0)
        r = prev * w[0:1] + up * w[1:2] + nxt * w[2:3] + cb_ref[:, c0:c0 + FF_CHUNK]
        return r[HALO:HALO + tm]

    acts = [(jax.nn.gelu(conv(c * FF_CHUNK)) * conv(D_FF + c * FF_CHUNK)).astype(BF16)
            for c in range(D_FF // FF_CHUNK)]
    acc = jnp.dot(jnp.concatenate(acts, axis=1), wdn_ref[...], preferred_element_type=F32)
    o_ref[0] = x + _rms(acc, gpost_ref[...])


def _ffn(x, g_pre, w_up, conv_w, conv_b, w_down, g_post, *, tm=1024):
    B, L, _ = x.shape
    n_tiles = L // tm
    hb = tm // HALO
    const = lambda a: pl.BlockSpec(a.shape, lambda b, i: (0,) * a.ndim, pipeline_mode=pl.Buffered(1))
    consts_a = (g_pre, w_up, conv_w, conv_b, w_down, g_post)
    return pl.pallas_call(
        functools.partial(_ffn_kernel, tm=tm, n_tiles=n_tiles),
        grid=(B, n_tiles),
        in_specs=[pl.BlockSpec((1, HALO, D_MODEL), lambda b, i: (b, jnp.maximum(i * hb - 1, 0), 0)),
                  pl.BlockSpec((1, tm, D_MODEL), lambda b, i: (b, i, 0)),
                  pl.BlockSpec((1, HALO, D_MODEL),
                               lambda b, i: (b, jnp.minimum((i + 1) * hb, L // HALO - 1), 0))]
        + [const(a) for a in consts_a],
        out_specs=pl.BlockSpec((1, tm, D_MODEL), lambda b, i: (b, i, 0)),
        out_shape=jax.ShapeDtypeStruct((B, L, D_MODEL), F32),
        compiler_params=pltpu.CompilerParams(
            dimension_semantics=("arbitrary", "arbitrary"), vmem_limit_bytes=VMEM_LIMIT),
        name="ffn",
    )(x, x, x, *consts_a)


def _layer(x, p):
    q, k, v, u, uf, ga, gs = _inproj(x, p['g_mix_pre'], p['w_in'])
    att = _attention(q, k, v, p['attn_bias'])
    yf = _ssm(uf, p['pp'], p['mq'], p['coef'])
    x1 = _merge(x, att, yf, u, ga, gs, p['ssm_d'], p['w_glu'], p['b_glu'], p['w_branch_att'],
                p['w_branch_ssm'], p['w_out'], p['g_mix_post'])
    return _ffn(x1, p['g_ffn_pre'], p['w_up'], p['conv_w'], p['conv_b'], p['w_down'], p['g_ffn_post'])


def _prepare(l, g_mix_pre, g_mix_post, w_in, attn_rpb, ssm_lam_re, ssm_lam_im, ssm_log_dt, ssm_b_re,
             ssm_b_im, ssm_c_re, ssm_c_im, ssm_d, w_glu, b_glu, w_branch_att, w_branch_ssm, w_out,
             g_ffn_pre, g_ffn_post, w_up, conv_w, conv_b, w_down):
    pp, mq, coef = _ssm_tables(ssm_lam_re[l], ssm_lam_im[l], ssm_log_dt[l], ssm_b_re[l], ssm_b_im[l],
                               ssm_c_re[l], ssm_c_im[l])
    vec = lambda a: a[l].astype(F32).reshape(1, -1)
    return dict(
        g_mix_pre=vec(g_mix_pre), g_mix_post=vec(g_mix_post), w_in=w_in[l].astype(BF16),
        attn_bias=_attn_bias_blocks(attn_rpb[l].astype(F32)), pp=pp, mq=mq, coef=coef, ssm_d=vec(ssm_d),
        w_glu=w_glu[l].astype(BF16), b_glu=vec(b_glu), w_branch_att=w_branch_att[l].astype(BF16),
        w_branch_ssm=w_branch_ssm[l].astype(BF16), w_out=w_out[l].astype(BF16),
        g_ffn_pre=vec(g_ffn_pre), g_ffn_post=vec(g_ffn_post), w_up=w_up[l].astype(BF16),
        conv_w=conv_w[l].astype(F32), conv_b=vec(conv_b), w_down=w_down[l].astype(BF16))


def kernel(x_prompt, x_sample, g_mix_pre, g_mix_post, w_in, attn_rpb, ssm_lam_re, ssm_lam_im, ssm_log_dt,
           ssm_b_re, ssm_b_im, ssm_c_re, ssm_c_im, ssm_d, w_glu, b_glu, w_branch_att, w_branch_ssm, w_out,
           g_ffn_pre, g_ffn_post, w_up, conv_w, conv_b, w_down):
    weights = (g_mix_pre, g_mix_post, w_in, attn_rpb, ssm_lam_re, ssm_lam_im, ssm_log_dt, ssm_b_re,
               ssm_b_im, ssm_c_re, ssm_c_im, ssm_d, w_glu, b_glu, w_branch_att, w_branch_ssm, w_out,
               g_ffn_pre, g_ffn_post, w_up, conv_w, conv_b, w_down)
    layers = [_prepare(l, *weights) for l in range(w_in.shape[0])]

    def trunk(x):
        for p in layers:
            x = _layer(x, p)
        return x

    return trunk(x_prompt), trunk(x_sample)
```

```python
import functools

import numpy as np
import jax
import jax.numpy as jnp
from jax import lax
from jax.experimental import pallas as pl
from jax.experimental.pallas import tpu as pltpu

D_MODEL = 1024
GRID_W = 64
N_HEADS = 8
HEAD_DIM = 64
D_ATT = N_HEADS * HEAD_DIM
NA_KH = 8
NA_KW = 16
SSM_GROUP = 16
D_SSM = 512
N_GROUPS = D_SSM // SSM_GROUP
STATE_P = 64
D_FF = 2816
D_IN = 3 * D_ATT + D_SSM + 2 * D_MODEL
EPS = 1e-6
NEG_BIG = -1e30

CHUNK = 16
FLAT = CHUNK * SSM_GROUP
Q_ROWS = 4
K_ROWS = 12
GROUPS_PER_STEP = 4
LANES = 128
VMEM_LIMIT = 56 * 1024 * 1024
ATTN_VMEM_LIMIT = 62 * 1024 * 1024

F32 = jnp.float32
BF16 = jnp.bfloat16


def _rms(x, g):
    return x * lax.rsqrt(jnp.mean(x * x, axis=-1, keepdims=True) + EPS) * g


def _block_transpose8(vs):
    lane = lax.broadcasted_iota(jnp.int32, vs[0].shape, 1)
    vs = list(vs)
    for d in (4, 2, 1):
        keep = (lane & (d * SSM_GROUP)) == 0
        for r in range(8):
            if r & d:
                continue
            lo, hi = vs[r], vs[r + d]
            vs[r] = jnp.where(keep, lo, pltpu.roll(hi, d * SSM_GROUP, 1))
            vs[r + d] = jnp.where(keep, pltpu.roll(lo, LANES - d * SSM_GROUP, 1), hi)
    return vs


def _sublane_transpose8(vs):
    sub = lax.broadcasted_iota(jnp.int32, vs[0].shape, 0)
    vs = list(vs)
    for d in (4, 2, 1):
        keep = (sub & d) == 0
        for r in range(8):
            if r & d:
                continue
            lo, hi = vs[r], vs[r + d]
            vs[r] = jnp.where(keep, lo, pltpu.roll(hi, d, 0))
            vs[r + d] = jnp.where(keep, pltpu.roll(lo, 8 - d, 0), hi)
    return vs


def _inproj_kernel(x_ref, g_ref, w_ref, q_ref, k_ref, v_ref, u_ref, uf_ref, ga_ref, gs_ref, u_sc, *, tm):
    x = x_ref[0]
    h = _rms(x, g_ref[...]).astype(BF16)

    def proj(c0, n):
        return jnp.dot(h, w_ref[:, c0:c0 + n], preferred_element_type=F32)

    q_ref[0] = (proj(0, D_ATT) * (HEAD_DIM ** -0.5)).astype(BF16)
    k_ref[0] = proj(D_ATT, D_ATT).astype(BF16)
    v_ref[0] = proj(2 * D_ATT, D_ATT).astype(BF16)
    u = proj(3 * D_ATT, D_SSM)
    u_ref[0] = u
    for jt in range(D_SSM // LANES):
        u_sc[jt] = u[:, jt * LANES:(jt + 1) * LANES]
    ga_ref[0] = proj(3 * D_ATT + D_SSM, D_MODEL).astype(BF16)
    gs_ref[0] = proj(3 * D_ATT + D_SSM + D_MODEL, D_MODEL).astype(BF16)

    n_rb = tm // (8 * CHUNK)
    pieces = [[[None, None] for _ in range(n_rb)] for _ in range(N_GROUPS)]
    for rb in range(n_rb):
        for jt in range(D_SSM // LANES):
            for half in range(2):
                vs = [u_sc[jt, pl.ds(rb * 8 * CHUNK + 8 * half + sl, 8, stride=CHUNK), :]
                      for sl in range(8)]
                out = _block_transpose8(vs)
                for gl in range(8):
                    pieces[8 * jt + gl][rb][half] = out[gl]
    for g in range(N_GROUPS):
        rows = [jnp.concatenate(pieces[g][rb], axis=1) for rb in range(n_rb)]
        uf_ref[0, g] = jnp.concatenate(rows, axis=0).astype(BF16)


def _inproj(x, g_pre, w_in, *, tm=512):
    B, L, _ = x.shape
    nc = L // CHUNK
    row = lambda n: pl.BlockSpec((1, tm, n), lambda b, i: (b, i, 0))
    const = lambda shape: pl.BlockSpec(shape, lambda b, i: (0,) * len(shape),
                                       pipeline_mode=pl.Buffered(1))
    return pl.pallas_call(
        functools.partial(_inproj_kernel, tm=tm),
        grid=(B, L // tm),
        in_specs=[row(D_MODEL), const((1, D_MODEL)), const((D_MODEL, D_IN))],
        out_specs=[row(D_ATT), row(D_ATT), row(D_ATT), row(D_SSM),
                   pl.BlockSpec((1, N_GROUPS, tm // CHUNK, FLAT), lambda b, i: (b, 0, i, 0)),
                   row(D_MODEL), row(D_MODEL)],
        out_shape=[jax.ShapeDtypeStruct((B, L, D_ATT), BF16)] * 3
        + [jax.ShapeDtypeStruct((B, L, D_SSM), F32),
           jax.ShapeDtypeStruct((B, N_GROUPS, nc, FLAT), BF16),
           jax.ShapeDtypeStruct((B, L, D_MODEL), BF16),
           jax.ShapeDtypeStruct((B, L, D_MODEL), BF16)],
        scratch_shapes=[pltpu.VMEM((D_SSM // LANES, tm, LANES), F32)],
        compiler_params=pltpu.CompilerParams(
            dimension_semantics=("arbitrary", "arbitrary"), vmem_limit_bytes=VMEM_LIMIT),
        name="inproj",
    )(x, g_pre, w_in)


N_DR = 2 * NA_KH - 1
_ROW_PATTERNS = ((0, (0, 0, 0, 0)), (4, (0, 1, 2, 3)), (8, (4, 4, 4, 4)))


def _attn_bias_blocks(rpb):
    qc = np.arange(GRID_W)[:, None]
    kc = np.arange(GRID_W)[None, :]
    col_start = np.clip(qc - NA_KW // 2, 0, GRID_W - NA_KW)
    col_ok = (kc >= col_start) & (kc < col_start + NA_KW)
    pad = GRID_W - NA_KW
    rp = jnp.pad(rpb, ((0, 0), (0, 0), (pad, pad)))
    tab = jnp.stack([rp[:, :, GRID_W - 1 - c:2 * GRID_W - 1 - c] for c in range(GRID_W)], axis=2)
    tab = jnp.where(col_ok[None, None], tab, NEG_BIG)
    tab = jnp.concatenate([tab, jnp.full((N_HEADS, 1, GRID_W, GRID_W), NEG_BIG, F32)], axis=1)
    return jnp.concatenate([tab, tab], axis=3).astype(F32)


def _assemble_bias(tab_ref, bias_sc):
    def per_head(h, carry):
        r0 = (h % 2) * (Q_ROWS * GRID_W)
        for p, (off, first_key) in enumerate(_ROW_PATTERNS):
            for i in range(Q_ROWS):
                for j in range(K_ROWS):
                    ok = first_key[i] <= j < first_key[i] + NA_KH
                    dr = j - i - off + NA_KH - 1 if ok else N_DR
                    half = slice((j % 2) * GRID_W, (j % 2 + 1) * GRID_W)
                    bias_sc[p, h // 2, pl.ds(pl.multiple_of(r0 + i * GRID_W, GRID_W), GRID_W),
                            j * GRID_W:(j + 1) * GRID_W] = tab_ref[h, dr, :, half]
        return carry
    lax.fori_loop(0, N_HEADS, per_head, 0)


def _attn_kernel(q_ref, k_ref, v_ref, tab_ref, o_ref, bias_ref, *, rows):
    n_groups = rows // Q_ROWS
    nq, nk = Q_ROWS * GRID_W, K_ROWS * GRID_W
    lane = lax.broadcasted_iota(jnp.int32, (nq, LANES), 1)

    @pl.when((pl.program_id(0) == 0) & (pl.program_id(1) == 0))
    def _():
        _assemble_bias(tab_ref, bias_ref)

    for gi in range(GROUPS_PER_STEP):
        g = pl.program_id(1) * GROUPS_PER_STEP + gi
        kr0 = jnp.clip(g * Q_ROWS - NA_KH // 2, 0, rows - K_ROWS)
        pat = jnp.where(g == 0, 0, jnp.where(g == n_groups - 1, 2, 1))
        start = pl.multiple_of(kr0 * GRID_W, Q_ROWS * GRID_W)
        qrows = slice(gi * nq, (gi + 1) * nq)
        for hp in range(N_HEADS // 2):
            cols = slice(hp * LANES, (hp + 1) * LANES)
            qp = q_ref[0, qrows, cols]
            kp = k_ref[0, pl.ds(start, nk), cols]
            vp = v_ref[0, pl.ds(start, nk), cols]
            q2 = jnp.concatenate([jnp.where(lane < HEAD_DIM, qp, jnp.zeros_like(qp)),
                                  jnp.where(lane >= HEAD_DIM, qp, jnp.zeros_like(qp))], axis=0)
            s = lax.dot_general(q2, kp, (((1,), (1,)), ((), ())), preferred_element_type=F32)
            s = s + bias_ref[pat, hp]
            m = jnp.max(s, axis=-1, keepdims=True)
            p = jnp.exp(s - m)
            l = jnp.sum(p, axis=-1, keepdims=True)
            o = jnp.dot(p.astype(BF16), vp, preferred_element_type=F32) / l
            o_ref[0, qrows, cols] = jnp.where(lane < HEAD_DIM, o[:nq], o[nq:]).astype(BF16)


def _attention(q, k, v, bias_blocks):
    B, L, _ = q.shape
    rows = L // GRID_W
    assert rows % (Q_ROWS * GROUPS_PER_STEP) == 0 and rows >= 3 * Q_ROWS
    nq = Q_ROWS * GRID_W * GROUPS_PER_STEP
    seq = pl.BlockSpec((1, L, D_ATT), lambda b, g: (b, 0, 0))
    return pl.pallas_call(
        functools.partial(_attn_kernel, rows=rows),
        grid=(B, rows // (Q_ROWS * GROUPS_PER_STEP)),
        in_specs=[pl.BlockSpec((1, nq, D_ATT), lambda b, g: (b, g, 0)), seq, seq,
                  pl.BlockSpec(bias_blocks.shape, lambda b, g: (0, 0, 0, 0), pipeline_mode=pl.Buffered(1))],
        out_specs=pl.BlockSpec((1, nq, D_ATT), lambda b, g: (b, g, 0)),
        out_shape=jax.ShapeDtypeStruct((B, L, D_ATT), BF16),
        scratch_shapes=[pltpu.VMEM((len(_ROW_PATTERNS), N_HEADS // 2, 2 * Q_ROWS * GRID_W, K_ROWS * GRID_W), F32)],
        compiler_params=pltpu.CompilerParams(
            dimension_semantics=("arbitrary", "arbitrary"), vmem_limit_bytes=ATTN_VMEM_LIMIT),
        name="attn",
    )(q, k, v, bias_blocks)


def _cmul(ar, ai, br, bi):
    return ar * br - ai * bi, ar * bi + ai * br


def _ssm_tables(lam_re, lam_im, log_dt, b_re, b_im, c_re, c_im):
    hp = lax.Precision.HIGHEST
    lam_re, lam_im = lam_re.astype(F32), lam_im.astype(F32)
    dt = jnp.exp(log_dt.astype(F32))[..., None]
    zr, zi = lam_re * dt, lam_im * dt
    d = jnp.arange(CHUNK + 1, dtype=F32)[:, None, None, None]
    mag = jnp.exp(zr[None] * d)
    pw_re, pw_im = mag * jnp.cos(zi[None] * d), mag * jnp.sin(zi[None] * d)
    nr, ni = pw_re[1] - 1.0, pw_im[1]
    den = lam_re * lam_re + lam_im * lam_im
    fr, fi = (nr * lam_re + ni * lam_im) / den, (ni * lam_re - nr * lam_im) / den
    bb_re, bb_im = _cmul(fr[..., None], fi[..., None], b_re.astype(F32), b_im.astype(F32))
    c_re, c_im = c_re.astype(F32), c_im.astype(F32)

    c_t_re, c_t_im = jnp.swapaxes(c_re, 2, 3)[..., None], jnp.swapaxes(c_im, 2, 3)[..., None]
    e_re, e_im = _cmul(c_t_re, c_t_im, bb_re[:, :, :, None, :], bb_im[:, :, :, None, :])
    e_re = e_re.reshape(2, N_GROUPS, STATE_P, SSM_GROUP * SSM_GROUP)
    e_im = e_im.reshape(2, N_GROUPS, STATE_P, SSM_GROUP * SSM_GROUP)
    kern = (jnp.einsum('dzgp,zgpn->dzgn', pw_re[:CHUNK], e_re, precision=hp)
            - jnp.einsum('dzgp,zgpn->dzgn', pw_im[:CHUNK], e_im, precision=hp))
    kf, kb = kern[:, 0], kern[:, 1]
    k_all = jnp.concatenate([kb[:0:-1], (kf[0] + kb[0])[None], kf[1:]], axis=0)
    k_t = jnp.transpose(k_all.reshape(2 * CHUNK - 1, N_GROUPS, SSM_GROUP, SSM_GROUP), (1, 3, 0, 2)).astype(BF16)
    m = jnp.stack([k_t[:, :, CHUNK - 1 - s:2 * CHUNK - 1 - s, :] for s in range(CHUNK)], axis=1)
    m = m.reshape(N_GROUPS, FLAT, FLAT)

    def state_in(z, powers):
        pr, pi = _cmul(pw_re[powers, z][:, :, :, None], pw_im[powers, z][:, :, :, None],
                       bb_re[z][None], bb_im[z][None])
        both = jnp.concatenate([pr, pi], axis=2).astype(BF16)
        return jnp.transpose(both, (1, 0, 3, 2)).reshape(N_GROUPS, FLAT, 2 * STATE_P)

    def state_out(z, powers):
        qr, qi = _cmul(c_re[z][None], c_im[z][None],
                       pw_re[powers, z][:, :, None, :], pw_im[powers, z][:, :, None, :])
        both = jnp.concatenate([qr, -qi], axis=3).astype(BF16)
        return jnp.transpose(both, (1, 3, 0, 2)).reshape(N_GROUPS, 2 * STATE_P, FLAT)

    fwd = np.arange(CHUNK)
    pp = jnp.concatenate([state_in(0, CHUNK - 1 - fwd), state_in(1, fwd)], axis=2)
    sq = jnp.concatenate([state_out(0, fwd + 1), state_out(1, CHUNK - fwd)], axis=1)
    ar, ai = pw_re[CHUNK], pw_im[CHUNK]
    ca = jnp.concatenate([ar, ar], axis=-1)
    cb = jnp.concatenate([-ai, ai], axis=-1)
    coef = jnp.stack([ca[0], cb[0], ca[1], cb[1]])
    return pp, m, sq, coef


def _ssm_kernel(uf_ref, pp_ref, m_ref, sq_ref, coef_ref, y_ref, s_ref, so_ref, *, nc):
    half = 2 * STATE_P
    for gb in range(N_GROUPS // 8):
        ds = [jnp.dot(uf_ref[0, 8 * gb + gl], pp_ref[8 * gb + gl], preferred_element_type=F32)
              for gl in range(8)]
        for z in range(2):
            for cb in range(nc // 8):
                out = _sublane_transpose8([d[cb * 8:(cb + 1) * 8, z * half:(z + 1) * half] for d in ds])
                for cl in range(8):
                    r0 = (cb * 8 + cl) * N_GROUPS + 8 * gb
                    s_ref[z, r0:r0 + 8, :] = out[cl]

    caf, cbf, cab, cbb = coef_ref[0], coef_ref[1], coef_ref[2], coef_ref[3]

    def step(c, carry):
        sf, sfx, sb, sbx = carry
        rf = pl.multiple_of(c * N_GROUPS, N_GROUPS)
        rb = pl.multiple_of((nc - 1 - c) * N_GROUPS, N_GROUPS)
        df = s_ref[0, pl.ds(rf, N_GROUPS), :]
        db = s_ref[1, pl.ds(rb, N_GROUPS), :]
        so_ref[0, pl.ds(rf, N_GROUPS), :] = sf
        so_ref[1, pl.ds(rb, N_GROUPS), :] = sb
        dfx = pltpu.roll(df, STATE_P, 1)
        dbx = pltpu.roll(db, STATE_P, 1)
        sf, sfx = caf * sf + cbf * sfx + df, caf * sfx - cbf * sf + dfx
        sb, sbx = cab * sb + cbb * sbx + db, cab * sbx - cbb * sb + dbx
        return sf, sfx, sb, sbx

    zero = jnp.zeros((N_GROUPS, half), F32)
    lax.fori_loop(0, nc, step, (zero, zero, zero, zero), unroll=8)

    for gb in range(N_GROUPS // 8):
        states = [[[], []] for _ in range(8)]
        for z in range(2):
            for cb in range(nc // 8):
                r0s = [(cb * 8 + cl) * N_GROUPS + 8 * gb for cl in range(8)]
                out = _sublane_transpose8([so_ref[z, r0:r0 + 8, :] for r0 in r0s])
                for gl in range(8):
                    states[gl][z].append(out[gl])
        for gl in range(8):
            g = 8 * gb + gl
            sf = jnp.concatenate(states[gl][0], axis=0).astype(BF16)
            sb = jnp.concatenate(states[gl][1], axis=0).astype(BF16)
            y_ref[0, g] = (jnp.dot(uf_ref[0, g], m_ref[g], preferred_element_type=F32)
                           + jnp.dot(jnp.concatenate([sf, sb], axis=1), sq_ref[g], preferred_element_type=F32))


def _ssm(uf, pp, m, sq, coef):
    B, _, nc, _ = uf.shape
    const = lambda a: pl.BlockSpec(a.shape, lambda b: (0,) * a.ndim, pipeline_mode=pl.Buffered(1))
    blk = pl.BlockSpec((1, N_GROUPS, nc, FLAT), lambda b: (b, 0, 0, 0))
    return pl.pallas_call(
        functools.partial(_ssm_kernel, nc=nc),
        grid=(B,),
        in_specs=[blk, const(pp), const(m), const(sq), const(coef)],
        out_specs=blk,
        out_shape=jax.ShapeDtypeStruct((B, N_GROUPS, nc, FLAT), F32),
        scratch_shapes=[pltpu.VMEM((2, nc * N_GROUPS, 2 * STATE_P), F32)] * 2,
        compiler_params=pltpu.CompilerParams(
            dimension_semantics=("arbitrary",), vmem_limit_bytes=VMEM_LIMIT),
        name="ssm",
    )(uf, pp, m, sq, coef)


MERGE_SUB = 512


def _merge_kernel(x_ref, att_ref, yf_ref, u_ref, ga_ref, gs_ref, d_ref, wglu_ref, bglu_ref,
                  watt_ref, wssm_ref, wout_ref, gpost_ref, o_ref, y_sc, *, tm):
    for rb in range(tm // (8 * CHUNK)):
        for jt in range(D_SSM // LANES):
            for half in range(2):
                vs = [yf_ref[0, 8 * jt + gl, rb * 8:(rb + 1) * 8, half * LANES:(half + 1) * LANES]
                      for gl in range(8)]
                out = _block_transpose8(vs)
                for sl in range(8):
                    y_sc[jt, pl.ds(rb * 8 * CHUNK + 8 * half + sl, 8, stride=CHUNK), :] = out[sl]

    for r0 in range(0, tm, MERGE_SUB):
        rows = slice(r0, r0 + MERGE_SUB)
        y = jnp.concatenate([y_sc[jt, rows, :] for jt in range(D_SSM // LANES)], axis=1)
        y = jax.nn.gelu(y + d_ref[...] * u_ref[0, rows, :])
        z = jnp.dot(y.astype(BF16), wglu_ref[...], preferred_element_type=F32) + bglu_ref[...]
        ssm = (y * jax.nn.sigmoid(z)).astype(BF16)
        a = jnp.dot(att_ref[0, rows, :], watt_ref[...], preferred_element_type=F32)
        b = jnp.dot(ssm, wssm_ref[...], preferred_element_type=F32)
        merged = (jax.nn.sigmoid(ga_ref[0, rows, :].astype(F32)) * a
                  + jax.nn.sigmoid(gs_ref[0, rows, :].astype(F32)) * b)
        mo = jnp.dot(merged.astype(BF16), wout_ref[...], preferred_element_type=F32)
        o_ref[0, rows, :] = x_ref[0, rows, :] + _rms(mo, gpost_ref[...])


def _merge(x, att, yf, u, ga, gs, d_skip, w_glu, b_glu, w_att, w_ssm, w_out, g_post, *, tm=1024):
    B, L, _ = x.shape
    row = lambda n: pl.BlockSpec((1, tm, n), lambda b, i: (b, i, 0))
    const = lambda a: pl.BlockSpec(a.shape, lambda b, i: (0,) * a.ndim, pipeline_mode=pl.Buffered(1))
    consts = (d_skip, w_glu, b_glu, w_att, w_ssm, w_out, g_post)
    return pl.pallas_call(
        functools.partial(_merge_kernel, tm=tm),
        grid=(B, L // tm),
        in_specs=[row(D_MODEL), row(D_ATT),
                  pl.BlockSpec((1, N_GROUPS, tm // CHUNK, FLAT), lambda b, i: (b, 0, i, 0)),
                  row(D_SSM), row(D_MODEL), row(D_MODEL)] + [const(a) for a in consts],
        out_specs=row(D_MODEL),
        out_shape=jax.ShapeDtypeStruct((B, L, D_MODEL), F32),
        scratch_shapes=[pltpu.VMEM((D_SSM // LANES, tm, LANES), F32)],
        compiler_params=pltpu.CompilerParams(
            dimension_semantics=("arbitrary", "arbitrary"), vmem_limit_bytes=VMEM_LIMIT),
        name="merge",
    )(x, att, yf, u, ga, gs, *consts)


HALO = 8
FF_CHUNK = 256


def _ffn_kernel(xp_ref, x_ref, xn_ref, gpre_ref, wup_ref, cw_ref, cb_ref, wdn_ref, gpost_ref, o_ref,
                *, tm, n_tiles):
    i = pl.program_id(1)
    x = x_ref[0]
    xp = jnp.where(i == 0, 0.0, xp_ref[0])
    xn = jnp.where(i == n_tiles - 1, 0.0, xn_ref[0])
    h = _rms(jnp.concatenate([xp, x, xn], axis=0), gpre_ref[...]).astype(BF16)
    n_ext = tm + 2 * HALO

    def conv(c0):
        up = jnp.dot(h, wup_ref[:, c0:c0 + FF_CHUNK], preferred_element_type=F32)
        w = cw_ref[:, c0:c0 + FF_CHUNK]
        prev = pltpu.roll(up, 1, 0)
        nxt = pltpu.roll(up, n_ext - 1, 0)
        r = prev * w[0:1] + up * w[1:2] + nxt * w[2:3] + cb_ref[:, c0:c0 + FF_CHUNK]
        return r[HALO:HALO + tm]

    acts = [(jax.nn.gelu(conv(c * FF_CHUNK)) * conv(D_FF + c * FF_CHUNK)).astype(BF16)
            for c in range(D_FF // FF_CHUNK)]
    acc = jnp.dot(jnp.concatenate(acts, axis=1), wdn_ref[...], preferred_element_type=F32)
    o_ref[0] = x + _rms(acc, gpost_ref[...])


def _ffn(x, g_pre, w_up, conv_w, conv_b, w_down, g_post, *, tm=1024):
    B, L, _ = x.shape
    n_tiles = L // tm
    hb = tm // HALO
    const = lambda a: pl.BlockSpec(a.shape, lambda b, i: (0,) * a.ndim, pipeline_mode=pl.Buffered(1))
    consts_a = (g_pre, w_up, conv_w, conv_b, w_down, g_post)
    return pl.pallas_call(
        functools.partial(_ffn_kernel, tm=tm, n_tiles=n_tiles),
        grid=(B, n_tiles),
        in_specs=[pl.BlockSpec((1, HALO, D_MODEL), lambda b, i: (b, jnp.maximum(i * hb - 1, 0), 0)),
                  pl.BlockSpec((1, tm, D_MODEL), lambda b, i: (b, i, 0)),
                  pl.BlockSpec((1, HALO, D_MODEL),
                               lambda b, i: (b, jnp.minimum((i + 1) * hb, L // HALO - 1), 0))]
        + [const(a) for a in consts_a],
        out_specs=pl.BlockSpec((1, tm, D_MODEL), lambda b, i: (b, i, 0)),
        out_shape=jax.ShapeDtypeStruct((B, L, D_MODEL), F32),
        compiler_params=pltpu.CompilerParams(
            dimension_semantics=("arbitrary", "arbitrary"), vmem_limit_bytes=VMEM_LIMIT),
        name="ffn",
    )(x, x, x, *consts_a)


def _layer(x, p):
    q, k, v, u, uf, ga, gs = _inproj(x, p['g_mix_pre'], p['w_in'])
    att = _attention(q, k, v, p['attn_bias'])
    yf = _ssm(uf, p['pp'], p['m'], p['sq'], p['coef'])
    x1 = _merge(x, att, yf, u, ga, gs, p['ssm_d'], p['w_glu'], p['b_glu'], p['w_branch_att'],
                p['w_branch_ssm'], p['w_out'], p['g_mix_post'])
    return _ffn(x1, p['g_ffn_pre'], p['w_up'], p['conv_w'], p['conv_b'], p['w_down'], p['g_ffn_post'])


def _prepare(l, g_mix_pre, g_mix_post, w_in, attn_rpb, ssm_lam_re, ssm_lam_im, ssm_log_dt, ssm_b_re,
             ssm_b_im, ssm_c_re, ssm_c_im, ssm_d, w_glu, b_glu, w_branch_att, w_branch_ssm, w_out,
             g_ffn_pre, g_ffn_post, w_up, conv_w, conv_b, w_down):
    pp, m, sq, coef = _ssm_tables(ssm_lam_re[l], ssm_lam_im[l], ssm_log_dt[l], ssm_b_re[l], ssm_b_im[l],
                               ssm_c_re[l], ssm_c_im[l])
    vec = lambda a: a[l].astype(F32).reshape(1, -1)
    return dict(
        g_mix_pre=vec(g_mix_pre), g_mix_post=vec(g_mix_post), w_in=w_in[l].astype(BF16),
        attn_bias=_attn_bias_blocks(attn_rpb[l].astype(F32)), pp=pp, m=m, sq=sq, coef=coef, ssm_d=vec(ssm_d),
        w_glu=w_glu[l].astype(BF16), b_glu=vec(b_glu), w_branch_att=w_branch_att[l].astype(BF16),
        w_branch_ssm=w_branch_ssm[l].astype(BF16), w_out=w_out[l].astype(BF16),
        g_ffn_pre=vec(g_ffn_pre), g_ffn_post=vec(g_ffn_post), w_up=w_up[l].astype(BF16),
        conv_w=conv_w[l].astype(F32), conv_b=vec(conv_b), w_down=w_down[l].astype(BF16))


def kernel(x_prompt, x_sample, g_mix_pre, g_mix_post, w_in, attn_rpb, ssm_lam_re, ssm_lam_im, ssm_log_dt,
           ssm_b_re, ssm_b_im, ssm_c_re, ssm_c_im, ssm_d, w_glu, b_glu, w_branch_att, w_branch_ssm, w_out,
           g_ffn_pre, g_ffn_post, w_up, conv_w, conv_b, w_down):
    weights = (g_mix_pre, g_mix_post, w_in, attn_rpb, ssm_lam_re, ssm_lam_im, ssm_log_dt, ssm_b_re,
               ssm_b_im, ssm_c_re, ssm_c_im, ssm_d, w_glu, b_glu, w_branch_att, w_branch_ssm, w_out,
               g_ffn_pre, g_ffn_post, w_up, conv_w, conv_b, w_down)
    layers = [_prepare(l, *weights) for l in range(w_in.shape[0])]

    def trunk(x):
        for p in layers:
            x = _layer(x, p)
        return x

    return trunk(x_prompt), trunk(x_sample)
```

```python
import functools

import numpy as np
import jax
import jax.numpy as jnp
from jax import lax
from jax.experimental import pallas as pl
from jax.experimental.pallas import tpu as pltpu

D_MODEL = 1024
GRID_W = 64
N_HEADS = 8
HEAD_DIM = 64
D_ATT = N_HEADS * HEAD_DIM
NA_KH = 8
NA_KW = 16
SSM_GROUP = 16
D_SSM = 512
N_GROUPS = D_SSM // SSM_GROUP
STATE_P = 64
D_FF = 2816
D_IN = 3 * D_ATT + D_SSM + 2 * D_MODEL
EPS = 1e-6
NEG_BIG = -1e30

CHUNK = 16
FLAT = CHUNK * SSM_GROUP
Q_ROWS = 4
K_ROWS = 12
GROUPS_PER_STEP = 4
LANES = 128
VMEM_LIMIT = 56 * 1024 * 1024
ATTN_VMEM_LIMIT = 62 * 1024 * 1024

F32 = jnp.float32
BF16 = jnp.bfloat16


def _rms(x, g):
    return x * lax.rsqrt(jnp.mean(x * x, axis=-1, keepdims=True) + EPS) * g


_GELU_C0 = float(np.sqrt(2.0 / np.pi))
_GELU_C1 = _GELU_C0 * 0.044715


def _gelu(x):
    return (0.5 * x) * (1.0 + jnp.tanh(x * (_GELU_C0 + _GELU_C1 * (x * x))))


def _block_transpose8(vs):
    lane = lax.broadcasted_iota(jnp.int32, vs[0].shape, 1)
    vs = list(vs)
    for d in (4, 2, 1):
        keep = (lane & (d * SSM_GROUP)) == 0
        for r in range(8):
            if r & d:
                continue
            lo, hi = vs[r], vs[r + d]
            vs[r] = jnp.where(keep, lo, pltpu.roll(hi, d * SSM_GROUP, 1))
            vs[r + d] = jnp.where(keep, pltpu.roll(lo, LANES - d * SSM_GROUP, 1), hi)
    return vs


def _sublane_transpose8(vs):
    sub = lax.broadcasted_iota(jnp.int32, vs[0].shape, 0)
    vs = list(vs)
    for d in (4, 2, 1):
        keep = (sub & d) == 0
        for r in range(8):
            if r & d:
                continue
            lo, hi = vs[r], vs[r + d]
            vs[r] = jnp.where(keep, lo, pltpu.roll(hi, d, 0))
            vs[r + d] = jnp.where(keep, pltpu.roll(lo, 8 - d, 0), hi)
    return vs


INPROJ_SUB = 512


def _inproj_kernel(x_ref, g_ref, w_ref, q_ref, k_ref, v_ref, u_ref, uf_ref, ga_ref, gs_ref, u_sc, *, tm):
    for r0 in range(0, tm, INPROJ_SUB):
        rows = slice(r0, r0 + INPROJ_SUB)
        h = _rms(x_ref[0, rows, :], g_ref[...]).astype(BF16)

        def proj(c0, n):
            return jnp.dot(h, w_ref[:, c0:c0 + n], preferred_element_type=F32)

        q_ref[0, rows, :] = (proj(0, D_ATT) * (HEAD_DIM ** -0.5)).astype(BF16)
        k_ref[0, rows, :] = proj(D_ATT, D_ATT).astype(BF16)
        v_ref[0, rows, :] = proj(2 * D_ATT, D_ATT).astype(BF16)
        u = proj(3 * D_ATT, D_SSM)
        u_ref[0, rows, :] = u
        for jt in range(D_SSM // LANES):
            u_sc[jt, rows, :] = u[:, jt * LANES:(jt + 1) * LANES]
        ga_ref[0, rows, :] = proj(3 * D_ATT + D_SSM, D_MODEL).astype(BF16)
        gs_ref[0, rows, :] = proj(3 * D_ATT + D_SSM + D_MODEL, D_MODEL).astype(BF16)

        n_rb = INPROJ_SUB // (8 * CHUNK)
        pieces = [[[None, None] for _ in range(n_rb)] for _ in range(N_GROUPS)]
        for rb in range(n_rb):
            for jt in range(D_SSM // LANES):
                for half in range(2):
                    vs = [u_sc[jt, pl.ds(r0 + rb * 8 * CHUNK + 8 * half + sl, 8, stride=CHUNK), :]
                          for sl in range(8)]
                    out = _block_transpose8(vs)
                    for gl in range(8):
                        pieces[8 * jt + gl][rb][half] = out[gl]
        for g in range(N_GROUPS):
            blocks = [jnp.concatenate(pieces[g][rb], axis=1) for rb in range(n_rb)]
            uf_ref[0, g, r0 // CHUNK:(r0 + INPROJ_SUB) // CHUNK, :] = jnp.concatenate(blocks, axis=0).astype(BF16)


def _inproj(x, g_pre, w_in, *, tm=1024):
    B, L, _ = x.shape
    nc = L // CHUNK
    row = lambda n: pl.BlockSpec((1, tm, n), lambda b, i: (b, i, 0))
    const = lambda shape: pl.BlockSpec(shape, lambda b, i: (0,) * len(shape),
                                       pipeline_mode=pl.Buffered(1))
    return pl.pallas_call(
        functools.partial(_inproj_kernel, tm=tm),
        grid=(B, L // tm),
        in_specs=[row(D_MODEL), const((1, D_MODEL)), const((D_MODEL, D_IN))],
        out_specs=[row(D_ATT), row(D_ATT), row(D_ATT), row(D_SSM),
                   pl.BlockSpec((1, N_GROUPS, tm // CHUNK, FLAT), lambda b, i: (b, 0, i, 0)),
                   row(D_MODEL), row(D_MODEL)],
        out_shape=[jax.ShapeDtypeStruct((B, L, D_ATT), BF16)] * 3
        + [jax.ShapeDtypeStruct((B, L, D_SSM), F32),
           jax.ShapeDtypeStruct((B, N_GROUPS, nc, FLAT), BF16),
           jax.ShapeDtypeStruct((B, L, D_MODEL), BF16),
           jax.ShapeDtypeStruct((B, L, D_MODEL), BF16)],
        scratch_shapes=[pltpu.VMEM((D_SSM // LANES, tm, LANES), F32)],
        compiler_params=pltpu.CompilerParams(
            dimension_semantics=("arbitrary", "arbitrary"), vmem_limit_bytes=VMEM_LIMIT),
        name="inproj",
    )(x, g_pre, w_in)


N_DR = 2 * NA_KH - 1
_ROW_PATTERNS = ((0, (0, 0, 0, 0)), (4, (0, 1, 2, 3)), (8, (4, 4, 4, 4)))


def _attn_bias_blocks(rpb):
    qc = np.arange(GRID_W)[:, None]
    kc = np.arange(GRID_W)[None, :]
    col_start = np.clip(qc - NA_KW // 2, 0, GRID_W - NA_KW)
    col_ok = (kc >= col_start) & (kc < col_start + NA_KW)
    pad = GRID_W - NA_KW
    rp = jnp.pad(rpb, ((0, 0), (0, 0), (pad, pad)))
    tab = jnp.stack([rp[:, :, GRID_W - 1 - c:2 * GRID_W - 1 - c] for c in range(GRID_W)], axis=2)
    tab = jnp.where(col_ok[None, None], tab, NEG_BIG)
    tab = jnp.concatenate([tab, jnp.full((N_HEADS, 1, GRID_W, GRID_W), NEG_BIG, F32)], axis=1)
    return jnp.concatenate([tab, tab], axis=3).astype(F32)


def _assemble_bias(tab_ref, bias_sc):
    def per_head(h, carry):
        r0 = (h % 2) * (Q_ROWS * GRID_W)
        for p, (off, first_key) in enumerate(_ROW_PATTERNS):
            for i in range(Q_ROWS):
                for j in range(K_ROWS):
                    ok = first_key[i] <= j < first_key[i] + NA_KH
                    dr = j - i - off + NA_KH - 1 if ok else N_DR
                    half = slice((j % 2) * GRID_W, (j % 2 + 1) * GRID_W)
                    bias_sc[p, h // 2, pl.ds(pl.multiple_of(r0 + i * GRID_W, GRID_W), GRID_W),
                            j * GRID_W:(j + 1) * GRID_W] = tab_ref[h, dr, :, half]
        return carry
    lax.fori_loop(0, N_HEADS, per_head, 0)


def _attn_kernel(q_ref, k_ref, v_ref, tab_ref, o_ref, bias_ref, *, rows):
    n_groups = rows // Q_ROWS
    nq, nk = Q_ROWS * GRID_W, K_ROWS * GRID_W
    lane = lax.broadcasted_iota(jnp.int32, (nq, LANES), 1)

    @pl.when((pl.program_id(0) == 0) & (pl.program_id(1) == 0))
    def _():
        _assemble_bias(tab_ref, bias_ref)

    for gi in range(GROUPS_PER_STEP):
        g = pl.program_id(1) * GROUPS_PER_STEP + gi
        kr0 = jnp.clip(g * Q_ROWS - NA_KH // 2, 0, rows - K_ROWS)
        pat = jnp.where(g == 0, 0, jnp.where(g == n_groups - 1, 2, 1))
        start = pl.multiple_of(kr0 * GRID_W, Q_ROWS * GRID_W)
        qrows = slice(gi * nq, (gi + 1) * nq)
        for hp in range(N_HEADS // 2):
            cols = slice(hp * LANES, (hp + 1) * LANES)
            qp = q_ref[0, qrows, cols]
            kp = k_ref[0, pl.ds(start, nk), cols]
            vp = v_ref[0, pl.ds(start, nk), cols]
            q2 = jnp.concatenate([jnp.where(lane < HEAD_DIM, qp, jnp.zeros_like(qp)),
                                  jnp.where(lane >= HEAD_DIM, qp, jnp.zeros_like(qp))], axis=0)
            s = lax.dot_general(q2, kp, (((1,), (1,)), ((), ())), preferred_element_type=F32)
            s = s + bias_ref[pat, hp]
            m = jnp.max(s, axis=-1, keepdims=True)
            p = jnp.exp(s - m)
            l = jnp.sum(p, axis=-1, keepdims=True)
            o = jnp.dot(p.astype(BF16), vp, preferred_element_type=F32) / l
            o_ref[0, qrows, cols] = jnp.where(lane < HEAD_DIM, o[:nq], o[nq:]).astype(BF16)


def _attention(q, k, v, bias_blocks):
    B, L, _ = q.shape
    rows = L // GRID_W
    assert rows % (Q_ROWS * GROUPS_PER_STEP) == 0 and rows >= 3 * Q_ROWS
    nq = Q_ROWS * GRID_W * GROUPS_PER_STEP
    seq = pl.BlockSpec((1, L, D_ATT), lambda b, g: (b, 0, 0))
    return pl.pallas_call(
        functools.partial(_attn_kernel, rows=rows),
        grid=(B, rows // (Q_ROWS * GROUPS_PER_STEP)),
        in_specs=[pl.BlockSpec((1, nq, D_ATT), lambda b, g: (b, g, 0)), seq, seq,
                  pl.BlockSpec(bias_blocks.shape, lambda b, g: (0, 0, 0, 0), pipeline_mode=pl.Buffered(1))],
        out_specs=pl.BlockSpec((1, nq, D_ATT), lambda b, g: (b, g, 0)),
        out_shape=jax.ShapeDtypeStruct((B, L, D_ATT), BF16),
        scratch_shapes=[pltpu.VMEM((len(_ROW_PATTERNS), N_HEADS // 2, 2 * Q_ROWS * GRID_W, K_ROWS * GRID_W), F32)],
        compiler_params=pltpu.CompilerParams(
            dimension_semantics=("arbitrary", "arbitrary"), vmem_limit_bytes=ATTN_VMEM_LIMIT),
        name="attn",
    )(q, k, v, bias_blocks)


def _cmul(ar, ai, br, bi):
    return ar * br - ai * bi, ar * bi + ai * br


def _ssm_tables(lam_re, lam_im, log_dt, b_re, b_im, c_re, c_im):
    hp = lax.Precision.HIGHEST
    lam_re, lam_im = lam_re.astype(F32), lam_im.astype(F32)
    dt = jnp.exp(log_dt.astype(F32))[..., None]
    zr, zi = lam_re * dt, lam_im * dt
    d = jnp.arange(CHUNK + 1, dtype=F32)[:, None, None, None]
    mag = jnp.exp(zr[None] * d)
    pw_re, pw_im = mag * jnp.cos(zi[None] * d), mag * jnp.sin(zi[None] * d)
    nr, ni = pw_re[1] - 1.0, pw_im[1]
    den = lam_re * lam_re + lam_im * lam_im
    fr, fi = (nr * lam_re + ni * lam_im) / den, (ni * lam_re - nr * lam_im) / den
    bb_re, bb_im = _cmul(fr[..., None], fi[..., None], b_re.astype(F32), b_im.astype(F32))
    c_re, c_im = c_re.astype(F32), c_im.astype(F32)

    c_t_re, c_t_im = jnp.swapaxes(c_re, 2, 3)[..., None], jnp.swapaxes(c_im, 2, 3)[..., None]
    e_re, e_im = _cmul(c_t_re, c_t_im, bb_re[:, :, :, None, :], bb_im[:, :, :, None, :])
    e_re = e_re.reshape(2, N_GROUPS, STATE_P, SSM_GROUP * SSM_GROUP)
    e_im = e_im.reshape(2, N_GROUPS, STATE_P, SSM_GROUP * SSM_GROUP)
    kern = (jnp.einsum('dzgp,zgpn->dzgn', pw_re[:CHUNK], e_re, precision=hp)
            - jnp.einsum('dzgp,zgpn->dzgn', pw_im[:CHUNK], e_im, precision=hp))
    kf, kb = kern[:, 0], kern[:, 1]
    k_all = jnp.concatenate([kb[:0:-1], (kf[0] + kb[0])[None], kf[1:]], axis=0)
    k_t = jnp.transpose(k_all.reshape(2 * CHUNK - 1, N_GROUPS, SSM_GROUP, SSM_GROUP), (1, 3, 0, 2)).astype(BF16)
    m = jnp.stack([k_t[:, :, CHUNK - 1 - s:2 * CHUNK - 1 - s, :] for s in range(CHUNK)], axis=1)
    m = m.reshape(N_GROUPS, FLAT, FLAT)

    def state_in(z, powers):
        pr, pi = _cmul(pw_re[powers, z][:, :, :, None], pw_im[powers, z][:, :, :, None],
                       bb_re[z][None], bb_im[z][None])
        both = jnp.concatenate([pr, pi], axis=2).astype(BF16)
        return jnp.transpose(both, (1, 0, 3, 2)).reshape(N_GROUPS, FLAT, 2 * STATE_P)

    def state_out(z, powers):
        qr, qi = _cmul(c_re[z][None], c_im[z][None],
                       pw_re[powers, z][:, :, None, :], pw_im[powers, z][:, :, None, :])
        both = jnp.concatenate([qr, -qi], axis=3).astype(BF16)
        return jnp.transpose(both, (1, 3, 0, 2)).reshape(N_GROUPS, 2 * STATE_P, FLAT)

    fwd = np.arange(CHUNK)
    pp = jnp.concatenate([state_in(0, CHUNK - 1 - fwd), state_in(1, fwd)], axis=2)
    sq = jnp.concatenate([state_out(0, fwd + 1), state_out(1, CHUNK - fwd)], axis=1)
    ar, ai = pw_re[CHUNK], pw_im[CHUNK]
    ca = jnp.concatenate([ar, ar], axis=-1)
    cb = jnp.concatenate([-ai, ai], axis=-1)
    coef = jnp.stack([ca[0], cb[0], ca[1], cb[1]])
    return pp, m, sq, coef


def _ssm_kernel(uf_ref, pp_ref, m_ref, sq_ref, coef_ref, y_ref, s_ref, so_ref, *, nc):
    half = 2 * STATE_P
    for gb in range(N_GROUPS // 8):
        ds = [jnp.dot(uf_ref[0, 8 * gb + gl], pp_ref[8 * gb + gl], preferred_element_type=F32)
              for gl in range(8)]
        for z in range(2):
            for cb in range(nc // 8):
                out = _sublane_transpose8([d[cb * 8:(cb + 1) * 8, z * half:(z + 1) * half] for d in ds])
                for cl in range(8):
                    r0 = (cb * 8 + cl) * N_GROUPS + 8 * gb
                    s_ref[z, r0:r0 + 8, :] = out[cl]

    caf, cbf, cab, cbb = coef_ref[0], coef_ref[1], coef_ref[2], coef_ref[3]

    def step(c, carry):
        sf, sfx, sb, sbx = carry
        rf = pl.multiple_of(c * N_GROUPS, N_GROUPS)
        rb = pl.multiple_of((nc - 1 - c) * N_GROUPS, N_GROUPS)
        df = s_ref[0, pl.ds(rf, N_GROUPS), :]
        db = s_ref[1, pl.ds(rb, N_GROUPS), :]
        so_ref[0, pl.ds(rf, N_GROUPS), :] = sf
        so_ref[1, pl.ds(rb, N_GROUPS), :] = sb
        dfx = pltpu.roll(df, STATE_P, 1)
        dbx = pltpu.roll(db, STATE_P, 1)
        sf, sfx = caf * sf + cbf * sfx + df, caf * sfx - cbf * sf + dfx
        sb, sbx = cab * sb + cbb * sbx + db, cab * sbx - cbb * sb + dbx
        return sf, sfx, sb, sbx

    zero = jnp.zeros((N_GROUPS, half), F32)
    lax.fori_loop(0, nc, step, (zero, zero, zero, zero), unroll=8)

    for gb in range(N_GROUPS // 8):
        states = [[[], []] for _ in range(8)]
        for z in range(2):
            for cb in range(nc // 8):
                r0s = [(cb * 8 + cl) * N_GROUPS + 8 * gb for cl in range(8)]
                out = _sublane_transpose8([so_ref[z, r0:r0 + 8, :] for r0 in r0s])
                for gl in range(8):
                    states[gl][z].append(out[gl])
        for gl in range(8):
            g = 8 * gb + gl
            sf = jnp.concatenate(states[gl][0], axis=0).astype(BF16)
            sb = jnp.concatenate(states[gl][1], axis=0).astype(BF16)
            y_ref[0, g] = (jnp.dot(uf_ref[0, g], m_ref[g], preferred_element_type=F32)
                           + jnp.dot(jnp.concatenate([sf, sb], axis=1), sq_ref[g], preferred_element_type=F32))


def _ssm(uf, pp, m, sq, coef):
    B, _, nc, _ = uf.shape
    const = lambda a: pl.BlockSpec(a.shape, lambda b: (0,) * a.ndim, pipeline_mode=pl.Buffered(1))
    blk = pl.BlockSpec((1, N_GROUPS, nc, FLAT), lambda b: (b, 0, 0, 0))
    return pl.pallas_call(
        functools.partial(_ssm_kernel, nc=nc),
        grid=(B,),
        in_specs=[blk, const(pp), const(m), const(sq), const(coef)],
        out_specs=blk,
        out_shape=jax.ShapeDtypeStruct((B, N_GROUPS, nc, FLAT), F32),
        scratch_shapes=[pltpu.VMEM((2, nc * N_GROUPS, 2 * STATE_P), F32)] * 2,
        compiler_params=pltpu.CompilerParams(
            dimension_semantics=("arbitrary",), vmem_limit_bytes=VMEM_LIMIT),
        name="ssm",
    )(uf, pp, m, sq, coef)


MERGE_SUB = 512


def _merge_kernel(x_ref, att_ref, yf_ref, u_ref, ga_ref, gs_ref, d_ref, wglu_ref, bglu_ref,
                  watt_ref, wssm_ref, wout_ref, gpost_ref, o_ref, y_sc, *, tm):
    for rb in range(tm // (8 * CHUNK)):
        for jt in range(D_SSM // LANES):
            for half in range(2):
                vs = [yf_ref[0, 8 * jt + gl, rb * 8:(rb + 1) * 8, half * LANES:(half + 1) * LANES]
                      for gl in range(8)]
                out = _block_transpose8(vs)
                for sl in range(8):
                    y_sc[jt, pl.ds(rb * 8 * CHUNK + 8 * half + sl, 8, stride=CHUNK), :] = out[sl]

    for r0 in range(0, tm, MERGE_SUB):
        rows = slice(r0, r0 + MERGE_SUB)
        y = jnp.concatenate([y_sc[jt, rows, :] for jt in range(D_SSM // LANES)], axis=1)
        y = _gelu(y + d_ref[...] * u_ref[0, rows, :])
        z = jnp.dot(y.astype(BF16), wglu_ref[...], preferred_element_type=F32) + bglu_ref[...]
        ssm = (y * jax.nn.sigmoid(z)).astype(BF16)
        a = jnp.dot(att_ref[0, rows, :], watt_ref[...], preferred_element_type=F32)
        b = jnp.dot(ssm, wssm_ref[...], preferred_element_type=F32)
        merged = (jax.nn.sigmoid(ga_ref[0, rows, :].astype(F32)) * a
                  + jax.nn.sigmoid(gs_ref[0, rows, :].astype(F32)) * b)
        mo = jnp.dot(merged.astype(BF16), wout_ref[...], preferred_element_type=F32)
        o_ref[0, rows, :] = x_ref[0, rows, :] + _rms(mo, gpost_ref[...])


def _merge(x, att, yf, u, ga, gs, d_skip, w_glu, b_glu, w_att, w_ssm, w_out, g_post, *, tm=1024):
    B, L, _ = x.shape
    row = lambda n: pl.BlockSpec((1, tm, n), lambda b, i: (b, i, 0))
    const = lambda a: pl.BlockSpec(a.shape, lambda b, i: (0,) * a.ndim, pipeline_mode=pl.Buffered(1))
    consts = (d_skip, w_glu, b_glu, w_att, w_ssm, w_out, g_post)
    return pl.pallas_call(
        functools.partial(_merge_kernel, tm=tm),
        grid=(B, L // tm),
        in_specs=[row(D_MODEL), row(D_ATT),
                  pl.BlockSpec((1, N_GROUPS, tm // CHUNK, FLAT), lambda b, i: (b, 0, i, 0)),
                  row(D_SSM), row(D_MODEL), row(D_MODEL)] + [const(a) for a in consts],
        out_specs=row(D_MODEL),
        out_shape=jax.ShapeDtypeStruct((B, L, D_MODEL), F32),
        scratch_shapes=[pltpu.VMEM((D_SSM // LANES, tm, LANES), F32)],
        compiler_params=pltpu.CompilerParams(
            dimension_semantics=("arbitrary", "arbitrary"), vmem_limit_bytes=VMEM_LIMIT),
        name="merge",
    )(x, att, yf, u, ga, gs, *consts)


HALO = 8
FF_CHUNK = 256


def _ffn_kernel(xp_ref, x_ref, xn_ref, gpre_ref, wup_ref, cw_ref, cb_ref, wdn_ref, gpost_ref, o_ref,
                *, tm, n_tiles):
    i = pl.program_id(1)
    x = x_ref[0]
    xp = jnp.where(i == 0, 0.0, xp_ref[0])
    xn = jnp.where(i == n_tiles - 1, 0.0, xn_ref[0])
    h = _rms(jnp.concatenate([xp, x, xn], axis=0), gpre_ref[...]).astype(BF16)
    n_ext = tm + 2 * HALO

    def conv(c0):
        up = jnp.dot(h, wup_ref[:, c0:c0 + FF_CHUNK], preferred_element_type=F32)
        w = cw_ref[:, c0:c0 + FF_CHUNK]
        prev = pltpu.roll(up, 1, 0)
        nxt = pltpu.roll(up, n_ext - 1, 0)
        r = prev * w[0:1] + up * w[1:2] + nxt * w[2:3] + cb_ref[:, c0:c0 + FF_CHUNK]
        return r[HALO:HALO + tm]

    acts = [(_gelu(conv(c * FF_CHUNK)) * conv(D_FF + c * FF_CHUNK)).astype(BF16)
            for c in range(D_FF // FF_CHUNK)]
    acc = jnp.dot(jnp.concatenate(acts, axis=1), wdn_ref[...], preferred_element_type=F32)
    o_ref[0] = x + _rms(acc, gpost_ref[...])


def _ffn(x, g_pre, w_up, conv_w, conv_b, w_down, g_post, *, tm=1024):
    B, L, _ = x.shape
    n_tiles = L // tm
    hb = tm // HALO
    const = lambda a: pl.BlockSpec(a.shape, lambda b, i: (0,) * a.ndim, pipeline_mode=pl.Buffered(1))
    consts_a = (g_pre, w_up, conv_w, conv_b, w_down, g_post)
    return pl.pallas_call(
        functools.partial(_ffn_kernel, tm=tm, n_tiles=n_tiles),
        grid=(B, n_tiles),
        in_specs=[pl.BlockSpec((1, HALO, D_MODEL), lambda b, i: (b, jnp.maximum(i * hb - 1, 0), 0)),
                  pl.BlockSpec((1, tm, D_MODEL), lambda b, i: (b, i, 0)),
                  pl.BlockSpec((1, HALO, D_MODEL),
                               lambda b, i: (b, jnp.minimum((i + 1) * hb, L // HALO - 1), 0))]
        + [const(a) for a in consts_a],
        out_specs=pl.BlockSpec((1, tm, D_MODEL), lambda b, i: (b, i, 0)),
        out_shape=jax.ShapeDtypeStruct((B, L, D_MODEL), F32),
        compiler_params=pltpu.CompilerParams(
            dimension_semantics=("arbitrary", "arbitrary"), vmem_limit_bytes=VMEM_LIMIT),
        name="ffn",
    )(x, x, x, *consts_a)


def _layer(x, p):
    q, k, v, u, uf, ga, gs = _inproj(x, p['g_mix_pre'], p['w_in'])
    att = _attention(q, k, v, p['attn_bias'])
    yf = _ssm(uf, p['pp'], p['m'], p['sq'], p['coef'])
    x1 = _merge(x, att, yf, u, ga, gs, p['ssm_d'], p['w_glu'], p['b_glu'], p['w_branch_att'],
                p['w_branch_ssm'], p['w_out'], p['g_mix_post'])
    return _ffn(x1, p['g_ffn_pre'], p['w_up'], p['conv_w'], p['conv_b'], p['w_down'], p['g_ffn_post'])


def _prepare(l, g_mix_pre, g_mix_post, w_in, attn_rpb, ssm_lam_re, ssm_lam_im, ssm_log_dt, ssm_b_re,
             ssm_b_im, ssm_c_re, ssm_c_im, ssm_d, w_glu, b_glu, w_branch_att, w_branch_ssm, w_out,
             g_ffn_pre, g_ffn_post, w_up, conv_w, conv_b, w_down):
    pp, m, sq, coef = _ssm_tables(ssm_lam_re[l], ssm_lam_im[l], ssm_log_dt[l], ssm_b_re[l], ssm_b_im[l],
                               ssm_c_re[l], ssm_c_im[l])
    vec = lambda a: a[l].astype(F32).reshape(1, -1)
    return dict(
        g_mix_pre=vec(g_mix_pre), g_mix_post=vec(g_mix_post), w_in=w_in[l].astype(BF16),
        attn_bias=_attn_bias_blocks(attn_rpb[l].astype(F32)), pp=pp, m=m, sq=sq, coef=coef, ssm_d=vec(ssm_d),
        w_glu=w_glu[l].astype(BF16), b_glu=vec(b_glu), w_branch_att=w_branch_att[l].astype(BF16),
        w_branch_ssm=w_branch_ssm[l].astype(BF16), w_out=w_out[l].astype(BF16),
        g_ffn_pre=vec(g_ffn_pre), g_ffn_post=vec(g_ffn_post), w_up=w_up[l].astype(BF16),
        conv_w=conv_w[l].astype(F32), conv_b=vec(conv_b), w_down=w_down[l].astype(BF16))


def kernel(x_prompt, x_sample, g_mix_pre, g_mix_post, w_in, attn_rpb, ssm_lam_re, ssm_lam_im, ssm_log_dt,
           ssm_b_re, ssm_b_im, ssm_c_re, ssm_c_im, ssm_d, w_glu, b_glu, w_branch_att, w_branch_ssm, w_out,
           g_ffn_pre, g_ffn_post, w_up, conv_w, conv_b, w_down):
    weights = (g_mix_pre, g_mix_post, w_in, attn_rpb, ssm_lam_re, ssm_lam_im, ssm_log_dt, ssm_b_re,
               ssm_b_im, ssm_c_re, ssm_c_im, ssm_d, w_glu, b_glu, w_branch_att, w_branch_ssm, w_out,
               g_ffn_pre, g_ffn_post, w_up, conv_w, conv_b, w_down)
    layers = [_prepare(l, *weights) for l in range(w_in.shape[0])]

    def trunk(x):
        for p in layers:
            x = _layer(x, p)
        return x

    return trunk(x_prompt), trunk(x_sample)
```

```python
import functools

import numpy as np
import jax
import jax.numpy as jnp
from jax import lax
from jax.experimental import pallas as pl
from jax.experimental.pallas import tpu as pltpu

D_MODEL = 1024
GRID_W = 64
N_HEADS = 8
HEAD_DIM = 64
D_ATT = N_HEADS * HEAD_DIM
NA_KH = 8
NA_KW = 16
SSM_GROUP = 16
D_SSM = 512
N_GROUPS = D_SSM // SSM_GROUP
STATE_P = 64
D_FF = 2816
D_IN = 3 * D_ATT + D_SSM + 2 * D_MODEL
EPS = 1e-6
NEG_BIG = -1e30

CHUNK = 16
FLAT = CHUNK * SSM_GROUP
SCAN_UNROLL = 32
Q_ROWS = 4
K_ROWS = 12
GROUPS_PER_STEP = 4
LANES = 128
VMEM_LIMIT = 56 * 1024 * 1024
ATTN_VMEM_LIMIT = 62 * 1024 * 1024

F32 = jnp.float32
BF16 = jnp.bfloat16


def _rms(x, g):
    return x * lax.rsqrt(jnp.mean(x * x, axis=-1, keepdims=True) + EPS) * g


_GELU_C0 = float(np.sqrt(2.0 / np.pi))
_GELU_C1 = _GELU_C0 * 0.044715


def _gelu(x):
    return (0.5 * x) * (1.0 + jnp.tanh(x * (_GELU_C0 + _GELU_C1 * (x * x))))


def _block_transpose8(vs):
    lane = lax.broadcasted_iota(jnp.int32, vs[0].shape, 1)
    vs = list(vs)
    for d in (4, 2, 1):
        keep = (lane & (d * SSM_GROUP)) == 0
        for r in range(8):
            if r & d:
                continue
            lo, hi = vs[r], vs[r + d]
            vs[r] = jnp.where(keep, lo, pltpu.roll(hi, d * SSM_GROUP, 1))
            vs[r + d] = jnp.where(keep, pltpu.roll(lo, LANES - d * SSM_GROUP, 1), hi)
    return vs


def _sublane_transpose8(vs):
    sub = lax.broadcasted_iota(jnp.int32, vs[0].shape, 0)
    vs = list(vs)
    for d in (4, 2, 1):
        keep = (sub & d) == 0
        for r in range(8):
            if r & d:
                continue
            lo, hi = vs[r], vs[r + d]
            vs[r] = jnp.where(keep, lo, pltpu.roll(hi, d, 0))
            vs[r + d] = jnp.where(keep, pltpu.roll(lo, 8 - d, 0), hi)
    return vs


INPROJ_SUB = 512


def _inproj_kernel(x_ref, g_ref, w_ref, q_ref, k_ref, v_ref, u_ref, uf_ref, ga_ref, gs_ref, u_sc, *, tm):
    for r0 in range(0, tm, INPROJ_SUB):
        rows = slice(r0, r0 + INPROJ_SUB)
        h = _rms(x_ref[0, rows, :], g_ref[...]).astype(BF16)

        def proj(c0, n):
            return jnp.dot(h, w_ref[:, c0:c0 + n], preferred_element_type=F32)

        q_ref[0, rows, :] = (proj(0, D_ATT) * (HEAD_DIM ** -0.5)).astype(BF16)
        k_ref[0, rows, :] = proj(D_ATT, D_ATT).astype(BF16)
        v_ref[0, rows, :] = proj(2 * D_ATT, D_ATT).astype(BF16)
        u = proj(3 * D_ATT, D_SSM)
        u_ref[0, rows, :] = u
        for jt in range(D_SSM // LANES):
            u_sc[jt, rows, :] = u[:, jt * LANES:(jt + 1) * LANES]
        ga_ref[0, rows, :] = proj(3 * D_ATT + D_SSM, D_MODEL).astype(BF16)
        gs_ref[0, rows, :] = proj(3 * D_ATT + D_SSM + D_MODEL, D_MODEL).astype(BF16)

        n_rb = INPROJ_SUB // (8 * CHUNK)
        pieces = [[[None, None] for _ in range(n_rb)] for _ in range(N_GROUPS)]
        for rb in range(n_rb):
            for jt in range(D_SSM // LANES):
                for half in range(2):
                    vs = [u_sc[jt, pl.ds(r0 + rb * 8 * CHUNK + 8 * half + sl, 8, stride=CHUNK), :]
                          for sl in range(8)]
                    out = _block_transpose8(vs)
                    for gl in range(8):
                        pieces[8 * jt + gl][rb][half] = out[gl]
        for g in range(N_GROUPS):
            blocks = [jnp.concatenate(pieces[g][rb], axis=1) for rb in range(n_rb)]
            uf_ref[0, g, r0 // CHUNK:(r0 + INPROJ_SUB) // CHUNK, :] = jnp.concatenate(blocks, axis=0).astype(BF16)


def _inproj(x, g_pre, w_in, *, tm=1024):
    B, L, _ = x.shape
    nc = L // CHUNK
    row = lambda n: pl.BlockSpec((1, tm, n), lambda b, i: (b, i, 0))
    const = lambda shape: pl.BlockSpec(shape, lambda b, i: (0,) * len(shape),
                                       pipeline_mode=pl.Buffered(1))
    return pl.pallas_call(
        functools.partial(_inproj_kernel, tm=tm),
        grid=(B, L // tm),
        in_specs=[row(D_MODEL), const((1, D_MODEL)), const((D_MODEL, D_IN))],
        out_specs=[row(D_ATT), row(D_ATT), row(D_ATT), row(D_SSM),
                   pl.BlockSpec((1, N_GROUPS, tm // CHUNK, FLAT), lambda b, i: (b, 0, i, 0)),
                   row(D_MODEL), row(D_MODEL)],
        out_shape=[jax.ShapeDtypeStruct((B, L, D_ATT), BF16)] * 3
        + [jax.ShapeDtypeStruct((B, L, D_SSM), F32),
           jax.ShapeDtypeStruct((B, N_GROUPS, nc, FLAT), BF16),
           jax.ShapeDtypeStruct((B, L, D_MODEL), BF16),
           jax.ShapeDtypeStruct((B, L, D_MODEL), BF16)],
        scratch_shapes=[pltpu.VMEM((D_SSM // LANES, tm, LANES), F32)],
        compiler_params=pltpu.CompilerParams(
            dimension_semantics=("arbitrary", "arbitrary"), vmem_limit_bytes=VMEM_LIMIT),
        name="inproj",
    )(x, g_pre, w_in)


N_DR = 2 * NA_KH - 1
_ROW_PATTERNS = ((0, (0, 0, 0, 0)), (4, (0, 1, 2, 3)), (8, (4, 4, 4, 4)))


def _attn_bias_blocks(rpb):
    qc = np.arange(GRID_W)[:, None]
    kc = np.arange(GRID_W)[None, :]
    col_start = np.clip(qc - NA_KW // 2, 0, GRID_W - NA_KW)
    col_ok = (kc >= col_start) & (kc < col_start + NA_KW)
    pad = GRID_W - NA_KW
    rp = jnp.pad(rpb, ((0, 0), (0, 0), (pad, pad)))
    tab = jnp.stack([rp[:, :, GRID_W - 1 - c:2 * GRID_W - 1 - c] for c in range(GRID_W)], axis=2)
    tab = jnp.where(col_ok[None, None], tab, NEG_BIG)
    tab = jnp.concatenate([tab, jnp.full((N_HEADS, 1, GRID_W, GRID_W), NEG_BIG, F32)], axis=1)
    return jnp.concatenate([tab, tab], axis=3).astype(F32)


def _assemble_bias(tab_ref, bias_sc):
    def per_head(h, carry):
        r0 = (h % 2) * (Q_ROWS * GRID_W)
        for p, (off, first_key) in enumerate(_ROW_PATTERNS):
            for i in range(Q_ROWS):
                for j in range(K_ROWS):
                    ok = first_key[i] <= j < first_key[i] + NA_KH
                    dr = j - i - off + NA_KH - 1 if ok else N_DR
                    half = slice((j % 2) * GRID_W, (j % 2 + 1) * GRID_W)
                    bias_sc[p, h // 2, pl.ds(pl.multiple_of(r0 + i * GRID_W, GRID_W), GRID_W),
                            j * GRID_W:(j + 1) * GRID_W] = tab_ref[h, dr, :, half]
        return carry
    lax.fori_loop(0, N_HEADS, per_head, 0)


def _attn_kernel(q_ref, k_ref, v_ref, tab_ref, o_ref, bias_ref, *, rows):
    n_groups = rows // Q_ROWS
    nq, nk = Q_ROWS * GRID_W, K_ROWS * GRID_W
    lane = lax.broadcasted_iota(jnp.int32, (nq, LANES), 1)

    @pl.when((pl.program_id(0) == 0) & (pl.program_id(1) == 0))
    def _():
        _assemble_bias(tab_ref, bias_ref)

    for gi in range(GROUPS_PER_STEP):
        g = pl.program_id(1) * GROUPS_PER_STEP + gi
        kr0 = jnp.clip(g * Q_ROWS - NA_KH // 2, 0, rows - K_ROWS)
        pat = jnp.where(g == 0, 0, jnp.where(g == n_groups - 1, 2, 1))
        start = pl.multiple_of(kr0 * GRID_W, Q_ROWS * GRID_W)
        qrows = slice(gi * nq, (gi + 1) * nq)
        for hp in range(N_HEADS // 2):
            cols = slice(hp * LANES, (hp + 1) * LANES)
            qp = q_ref[0, qrows, cols]
            kp = k_ref[0, pl.ds(start, nk), cols]
            vp = v_ref[0, pl.ds(start, nk), cols]
            q2 = jnp.concatenate([jnp.where(lane < HEAD_DIM, qp, jnp.zeros_like(qp)),
                                  jnp.where(lane >= HEAD_DIM, qp, jnp.zeros_like(qp))], axis=0)
            s = lax.dot_general(q2, kp, (((1,), (1,)), ((), ())), preferred_element_type=F32)
            s = s + bias_ref[pat, hp]
            m = jnp.max(s, axis=-1, keepdims=True)
            p = jnp.exp(s - m)
            l = jnp.sum(p, axis=-1, keepdims=True)
            o = jnp.dot(p.astype(BF16), vp, preferred_element_type=F32) / l
            o_ref[0, qrows, cols] = jnp.where(lane < HEAD_DIM, o[:nq], o[nq:]).astype(BF16)


def _attention(q, k, v, bias_blocks):
    B, L, _ = q.shape
    rows = L // GRID_W
    assert rows % (Q_ROWS * GROUPS_PER_STEP) == 0 and rows >= 3 * Q_ROWS
    nq = Q_ROWS * GRID_W * GROUPS_PER_STEP
    seq = pl.BlockSpec((1, L, D_ATT), lambda b, g: (b, 0, 0))
    return pl.pallas_call(
        functools.partial(_attn_kernel, rows=rows),
        grid=(B, rows // (Q_ROWS * GROUPS_PER_STEP)),
        in_specs=[pl.BlockSpec((1, nq, D_ATT), lambda b, g: (b, g, 0)), seq, seq,
                  pl.BlockSpec(bias_blocks.shape, lambda b, g: (0, 0, 0, 0), pipeline_mode=pl.Buffered(1))],
        out_specs=pl.BlockSpec((1, nq, D_ATT), lambda b, g: (b, g, 0)),
        out_shape=jax.ShapeDtypeStruct((B, L, D_ATT), BF16),
        scratch_shapes=[pltpu.VMEM((len(_ROW_PATTERNS), N_HEADS // 2, 2 * Q_ROWS * GRID_W, K_ROWS * GRID_W), F32)],
        compiler_params=pltpu.CompilerParams(
            dimension_semantics=("arbitrary", "arbitrary"), vmem_limit_bytes=ATTN_VMEM_LIMIT),
        name="attn",
    )(q, k, v, bias_blocks)


def _cmul(ar, ai, br, bi):
    return ar * br - ai * bi, ar * bi + ai * br


def _ssm_tables(lam_re, lam_im, log_dt, b_re, b_im, c_re, c_im):
    hp = lax.Precision.HIGHEST
    lam_re, lam_im = lam_re.astype(F32), lam_im.astype(F32)
    dt = jnp.exp(log_dt.astype(F32))[..., None]
    zr, zi = lam_re * dt, lam_im * dt
    d = jnp.arange(CHUNK + 1, dtype=F32)[:, None, None, None]
    mag = jnp.exp(zr[None] * d)
    pw_re, pw_im = mag * jnp.cos(zi[None] * d), mag * jnp.sin(zi[None] * d)
    nr, ni = pw_re[1] - 1.0, pw_im[1]
    den = lam_re * lam_re + lam_im * lam_im
    fr, fi = (nr * lam_re + ni * lam_im) / den, (ni * lam_re - nr * lam_im) / den
    bb_re, bb_im = _cmul(fr[..., None], fi[..., None], b_re.astype(F32), b_im.astype(F32))
    c_re, c_im = c_re.astype(F32), c_im.astype(F32)

    c_t_re, c_t_im = jnp.swapaxes(c_re, 2, 3)[..., None], jnp.swapaxes(c_im, 2, 3)[..., None]
    e_re, e_im = _cmul(c_t_re, c_t_im, bb_re[:, :, :, None, :], bb_im[:, :, :, None, :])
    e_re = e_re.reshape(2, N_GROUPS, STATE_P, SSM_GROUP * SSM_GROUP)
    e_im = e_im.reshape(2, N_GROUPS, STATE_P, SSM_GROUP * SSM_GROUP)
    kern = (jnp.einsum('dzgp,zgpn->dzgn', pw_re[:CHUNK], e_re, precision=hp)
            - jnp.einsum('dzgp,zgpn->dzgn', pw_im[:CHUNK], e_im, precision=hp))
    kf, kb = kern[:, 0], kern[:, 1]
    k_all = jnp.concatenate([kb[:0:-1], (kf[0] + kb[0])[None], kf[1:]], axis=0)
    k_t = jnp.transpose(k_all.reshape(2 * CHUNK - 1, N_GROUPS, SSM_GROUP, SSM_GROUP), (1, 3, 0, 2)).astype(BF16)
    m = jnp.stack([k_t[:, :, CHUNK - 1 - s:2 * CHUNK - 1 - s, :] for s in range(CHUNK)], axis=1)
    m = m.reshape(N_GROUPS, FLAT, FLAT)

    def state_in(z, powers):
        pr, pi = _cmul(pw_re[powers, z][:, :, :, None], pw_im[powers, z][:, :, :, None],
                       bb_re[z][None], bb_im[z][None])
        both = jnp.concatenate([pr, pi], axis=2).astype(BF16)
        return jnp.transpose(both, (1, 0, 3, 2)).reshape(N_GROUPS, FLAT, 2 * STATE_P)

    def state_out(z, powers):
        qr, qi = _cmul(c_re[z][None], c_im[z][None],
                       pw_re[powers, z][:, :, None, :], pw_im[powers, z][:, :, None, :])
        both = jnp.concatenate([qr, -qi], axis=3).astype(BF16)
        return jnp.transpose(both, (1, 3, 0, 2)).reshape(N_GROUPS, 2 * STATE_P, FLAT)

    fwd = np.arange(CHUNK)
    pp = jnp.concatenate([state_in(0, CHUNK - 1 - fwd), state_in(1, fwd)], axis=2)
    sq = jnp.concatenate([state_out(0, fwd + 1), state_out(1, CHUNK - fwd)], axis=1)
    ar, ai = pw_re[CHUNK], pw_im[CHUNK]
    ca = jnp.concatenate([ar, ar], axis=-1)
    cb = jnp.concatenate([-ai, ai], axis=-1)
    coef = jnp.stack([ca[0], cb[0], ca[1], cb[1]])
    return pp, m, sq, coef


def _ssm_kernel(uf_ref, pp_ref, m_ref, sq_ref, coef_ref, y_ref, s_ref, so_ref, *, nc):
    half = 2 * STATE_P
    for gb in range(N_GROUPS // 8):
        ds = [jnp.dot(uf_ref[0, 8 * gb + gl], pp_ref[8 * gb + gl], preferred_element_type=F32)
              for gl in range(8)]
        for z in range(2):
            for cb in range(nc // 8):
                out = _sublane_transpose8([d[cb * 8:(cb + 1) * 8, z * half:(z + 1) * half] for d in ds])
                for cl in range(8):
                    r0 = (cb * 8 + cl) * N_GROUPS + 8 * gb
                    s_ref[z, r0:r0 + 8, :] = out[cl]

    caf, cbf, cab, cbb = coef_ref[0], coef_ref[1], coef_ref[2], coef_ref[3]

    def step(c, carry):
        sf, sfx, sb, sbx = carry
        rf = pl.multiple_of(c * N_GROUPS, N_GROUPS)
        rb = pl.multiple_of((nc - 1 - c) * N_GROUPS, N_GROUPS)
        df = s_ref[0, pl.ds(rf, N_GROUPS), :]
        db = s_ref[1, pl.ds(rb, N_GROUPS), :]
        so_ref[0, pl.ds(rf, N_GROUPS), :] = sf
        so_ref[1, pl.ds(rb, N_GROUPS), :] = sb
        dfx = pltpu.roll(df, STATE_P, 1)
        dbx = pltpu.roll(db, STATE_P, 1)
        sf, sfx = caf * sf + cbf * sfx + df, caf * sfx - cbf * sf + dfx
        sb, sbx = cab * sb + cbb * sbx + db, cab * sbx - cbb * sb + dbx
        return sf, sfx, sb, sbx

    zero = jnp.zeros((N_GROUPS, half), F32)
    lax.fori_loop(0, nc, step, (zero, zero, zero, zero), unroll=SCAN_UNROLL)

    for gb in range(N_GROUPS // 8):
        states = [[[], []] for _ in range(8)]
        for z in range(2):
            for cb in range(nc // 8):
                r0s = [(cb * 8 + cl) * N_GROUPS + 8 * gb for cl in range(8)]
                out = _sublane_transpose8([so_ref[z, r0:r0 + 8, :] for r0 in r0s])
                for gl in range(8):
                    states[gl][z].append(out[gl])
        for gl in range(8):
            g = 8 * gb + gl
            sf = jnp.concatenate(states[gl][0], axis=0).astype(BF16)
            sb = jnp.concatenate(states[gl][1], axis=0).astype(BF16)
            y_ref[0, g] = (jnp.dot(uf_ref[0, g], m_ref[g], preferred_element_type=F32)
                           + jnp.dot(jnp.concatenate([sf, sb], axis=1), sq_ref[g], preferred_element_type=F32))


def _ssm(uf, pp, m, sq, coef):
    B, _, nc, _ = uf.shape
    const = lambda a: pl.BlockSpec(a.shape, lambda b: (0,) * a.ndim, pipeline_mode=pl.Buffered(1))
    blk = pl.BlockSpec((1, N_GROUPS, nc, FLAT), lambda b: (b, 0, 0, 0))
    return pl.pallas_call(
        functools.partial(_ssm_kernel, nc=nc),
        grid=(B,),
        in_specs=[blk, const(pp), const(m), const(sq), const(coef)],
        out_specs=blk,
        out_shape=jax.ShapeDtypeStruct((B, N_GROUPS, nc, FLAT), F32),
        scratch_shapes=[pltpu.VMEM((2, nc * N_GROUPS, 2 * STATE_P), F32)] * 2,
        compiler_params=pltpu.CompilerParams(
            dimension_semantics=("arbitrary",), vmem_limit_bytes=VMEM_LIMIT),
        name="ssm",
    )(uf, pp, m, sq, coef)


MERGE_SUB = 512


def _merge_kernel(x_ref, att_ref, yf_ref, u_ref, ga_ref, gs_ref, d_ref, wglu_ref, bglu_ref,
                  watt_ref, wssm_ref, wout_ref, gpost_ref, o_ref, y_sc, *, tm):
    for rb in range(tm // (8 * CHUNK)):
        for jt in range(D_SSM // LANES):
            for half in range(2):
                vs = [yf_ref[0, 8 * jt + gl, rb * 8:(rb + 1) * 8, half * LANES:(half + 1) * LANES]
                      for gl in range(8)]
                out = _block_transpose8(vs)
                for sl in range(8):
                    y_sc[jt, pl.ds(rb * 8 * CHUNK + 8 * half + sl, 8, stride=CHUNK), :] = out[sl]

    for r0 in range(0, tm, MERGE_SUB):
        rows = slice(r0, r0 + MERGE_SUB)
        y = jnp.concatenate([y_sc[jt, rows, :] for jt in range(D_SSM // LANES)], axis=1)
        y = _gelu(y + d_ref[...] * u_ref[0, rows, :])
        z = jnp.dot(y.astype(BF16), wglu_ref[...], preferred_element_type=F32) + bglu_ref[...]
        ssm = (y * jax.nn.sigmoid(z)).astype(BF16)
        a = jnp.dot(att_ref[0, rows, :], watt_ref[...], preferred_element_type=F32)
        b = jnp.dot(ssm, wssm_ref[...], preferred_element_type=F32)
        merged = (jax.nn.sigmoid(ga_ref[0, rows, :].astype(F32)) * a
                  + jax.nn.sigmoid(gs_ref[0, rows, :].astype(F32)) * b)
        mo = jnp.dot(merged.astype(BF16), wout_ref[...], preferred_element_type=F32)
        o_ref[0, rows, :] = x_ref[0, rows, :] + _rms(mo, gpost_ref[...])


def _merge(x, att, yf, u, ga, gs, d_skip, w_glu, b_glu, w_att, w_ssm, w_out, g_post, *, tm=1024):
    B, L, _ = x.shape
    row = lambda n: pl.BlockSpec((1, tm, n), lambda b, i: (b, i, 0))
    const = lambda a: pl.BlockSpec(a.shape, lambda b, i: (0,) * a.ndim, pipeline_mode=pl.Buffered(1))
    consts = (d_skip, w_glu, b_glu, w_att, w_ssm, w_out, g_post)
    return pl.pallas_call(
        functools.partial(_merge_kernel, tm=tm),
        grid=(B, L // tm),
        in_specs=[row(D_MODEL), row(D_ATT),
                  pl.BlockSpec((1, N_GROUPS, tm // CHUNK, FLAT), lambda b, i: (b, 0, i, 0)),
                  row(D_SSM), row(D_MODEL), row(D_MODEL)] + [const(a) for a in consts],
        out_specs=row(D_MODEL),
        out_shape=jax.ShapeDtypeStruct((B, L, D_MODEL), F32),
        scratch_shapes=[pltpu.VMEM((D_SSM // LANES, tm, LANES), F32)],
        compiler_params=pltpu.CompilerParams(
            dimension_semantics=("arbitrary", "arbitrary"), vmem_limit_bytes=VMEM_LIMIT),
        name="merge",
    )(x, att, yf, u, ga, gs, *consts)


HALO = 8
FF_CHUNK = 256


def _ffn_kernel(xp_ref, x_ref, xn_ref, gpre_ref, wup_ref, cw_ref, cb_ref, wdn_ref, gpost_ref, o_ref,
                *, tm, n_tiles):
    i = pl.program_id(1)
    x = x_ref[0]
    xp = jnp.where(i == 0, 0.0, xp_ref[0])
    xn = jnp.where(i == n_tiles - 1, 0.0, xn_ref[0])
    h = _rms(jnp.concatenate([xp, x, xn], axis=0), gpre_ref[...]).astype(BF16)
    n_ext = tm + 2 * HALO

    def conv(c0):
        up = jnp.dot(h, wup_ref[:, c0:c0 + FF_CHUNK], preferred_element_type=F32)
        w = cw_ref[:, c0:c0 + FF_CHUNK]
        prev = pltpu.roll(up, 1, 0)
        nxt = pltpu.roll(up, n_ext - 1, 0)
        r = prev * w[0:1] + up * w[1:2] + nxt * w[2:3] + cb_ref[:, c0:c0 + FF_CHUNK]
        return r[HALO:HALO + tm]

    acts = [(_gelu(conv(c * FF_CHUNK)) * conv(D_FF + c * FF_CHUNK)).astype(BF16)
            for c in range(D_FF // FF_CHUNK)]
    acc = jnp.dot(jnp.concatenate(acts, axis=1), wdn_ref[...], preferred_element_type=F32)
    o_ref[0] = x + _rms(acc, gpost_ref[...])


def _ffn(x, g_pre, w_up, conv_w, conv_b, w_down, g_post, *, tm=1024):
    B, L, _ = x.shape
    n_tiles = L // tm
    hb = tm // HALO
    const = lambda a: pl.BlockSpec(a.shape, lambda b, i: (0,) * a.ndim, pipeline_mode=pl.Buffered(1))
    consts_a = (g_pre, w_up, conv_w, conv_b, w_down, g_post)
    return pl.pallas_call(
        functools.partial(_ffn_kernel, tm=tm, n_tiles=n_tiles),
        grid=(B, n_tiles),
        in_specs=[pl.BlockSpec((1, HALO, D_MODEL), lambda b, i: (b, jnp.maximum(i * hb - 1, 0), 0)),
                  pl.BlockSpec((1, tm, D_MODEL), lambda b, i: (b, i, 0)),
                  pl.BlockSpec((1, HALO, D_MODEL),
                               lambda b, i: (b, jnp.minimum((i + 1) * hb, L // HALO - 1), 0))]
        + [const(a) for a in consts_a],
        out_specs=pl.BlockSpec((1, tm, D_MODEL), lambda b, i: (b, i, 0)),
        out_shape=jax.ShapeDtypeStruct((B, L, D_MODEL), F32),
        compiler_params=pltpu.CompilerParams(
            dimension_semantics=("arbitrary", "arbitrary"), vmem_limit_bytes=VMEM_LIMIT),
        name="ffn",
    )(x, x, x, *consts_a)


def _layer(x, p):
    q, k, v, u, uf, ga, gs = _inproj(x, p['g_mix_pre'], p['w_in'])
    att = _attention(q, k, v, p['attn_bias'])
    yf = _ssm(uf, p['pp'], p['m'], p['sq'], p['coef'])
    x1 = _merge(x, att, yf, u, ga, gs, p['ssm_d'], p['w_glu'], p['b_glu'], p['w_branch_att'],
                p['w_branch_ssm'], p['w_out'], p['g_mix_post'])
    return _ffn(x1, p['g_ffn_pre'], p['w_up'], p['conv_w'], p['conv_b'], p['w_down'], p['g_ffn_post'])


def _prepare(l, g_mix_pre, g_mix_post, w_in, attn_rpb, ssm_lam_re, ssm_lam_im, ssm_log_dt, ssm_b_re,
             ssm_b_im, ssm_c_re, ssm_c_im, ssm_d, w_glu, b_glu, w_branch_att, w_branch_ssm, w_out,
             g_ffn_pre, g_ffn_post, w_up, conv_w, conv_b, w_down):
    pp, m, sq, coef = _ssm_tables(ssm_lam_re[l], ssm_lam_im[l], ssm_log_dt[l], ssm_b_re[l], ssm_b_im[l],
                               ssm_c_re[l], ssm_c_im[l])
    vec = lambda a: a[l].astype(F32).reshape(1, -1)
    return dict(
        g_mix_pre=vec(g_mix_pre), g_mix_post=vec(g_mix_post), w_in=w_in[l].astype(BF16),
        attn_bias=_attn_bias_blocks(attn_rpb[l].astype(F32)), pp=pp, m=m, sq=sq, coef=coef, ssm_d=vec(ssm_d),
        w_glu=w_glu[l].astype(BF16), b_glu=vec(b_glu), w_branch_att=w_branch_att[l].astype(BF16),
        w_branch_ssm=w_branch_ssm[l].astype(BF16), w_out=w_out[l].astype(BF16),
        g_ffn_pre=vec(g_ffn_pre), g_ffn_post=vec(g_ffn_post), w_up=w_up[l].astype(BF16),
        conv_w=conv_w[l].astype(F32), conv_b=vec(conv_b), w_down=w_down[l].astype(BF16))


def kernel(x_prompt, x_sample, g_mix_pre, g_mix_post, w_in, attn_rpb, ssm_lam_re, ssm_lam_im, ssm_log_dt,
           ssm_b_re, ssm_b_im, ssm_c_re, ssm_c_im, ssm_d, w_glu, b_glu, w_branch_att, w_branch_ssm, w_out,
           g_ffn_pre, g_ffn_post, w_up, conv_w, conv_b, w_down):
    weights = (g_mix_pre, g_mix_post, w_in, attn_rpb, ssm_lam_re, ssm_lam_im, ssm_log_dt, ssm_b_re,
               ssm_b_im, ssm_c_re, ssm_c_im, ssm_d, w_glu, b_glu, w_branch_att, w_branch_ssm, w_out,
               g_ffn_pre, g_ffn_post, w_up, conv_w, conv_b, w_down)
    layers = [_prepare(l, *weights) for l in range(w_in.shape[0])]

    def trunk(x):
        for p in layers:
            x = _layer(x, p)
        return x

    return trunk(x_prompt), trunk(x_sample)
```

```python
import functools

import numpy as np
import jax
import jax.numpy as jnp
from jax import lax
from jax.experimental import pallas as pl
from jax.experimental.pallas import tpu as pltpu

D_MODEL = 1024
GRID_W = 64
N_HEADS = 8
HEAD_DIM = 64
D_ATT = N_HEADS * HEAD_DIM
NA_KH = 8
NA_KW = 16
SSM_GROUP = 16
D_SSM = 512
N_GROUPS = D_SSM // SSM_GROUP
STATE_P = 64
D_FF = 2816
D_IN = 3 * D_ATT + D_SSM + 2 * D_MODEL
EPS = 1e-6
NEG_BIG = -1e30

CHUNK = 16
FLAT = CHUNK * SSM_GROUP
SCAN_UNROLL = 32
Q_ROWS = 4
K_ROWS = 12
GROUPS_PER_STEP = 4
LANES = 128
VMEM_LIMIT = 56 * 1024 * 1024
ATTN_VMEM_LIMIT = 62 * 1024 * 1024

F32 = jnp.float32
BF16 = jnp.bfloat16


def _rms(x, g):
    return x * lax.rsqrt(jnp.mean(x * x, axis=-1, keepdims=True) + EPS) * g


_GELU_C0 = float(np.sqrt(2.0 / np.pi))
_GELU_C1 = _GELU_C0 * 0.044715


def _gelu(x):
    return (0.5 * x) * (1.0 + jnp.tanh(x * (_GELU_C0 + _GELU_C1 * (x * x))))


def _block_transpose8(vs):
    lane = lax.broadcasted_iota(jnp.int32, vs[0].shape, 1)
    vs = list(vs)
    for d in (4, 2, 1):
        keep = (lane & (d * SSM_GROUP)) == 0
        for r in range(8):
            if r & d:
                continue
            lo, hi = vs[r], vs[r + d]
            vs[r] = jnp.where(keep, lo, pltpu.roll(hi, d * SSM_GROUP, 1))
            vs[r + d] = jnp.where(keep, pltpu.roll(lo, LANES - d * SSM_GROUP, 1), hi)
    return vs


def _sublane_transpose8(vs):
    sub = lax.broadcasted_iota(jnp.int32, vs[0].shape, 0)
    vs = list(vs)
    for d in (4, 2, 1):
        keep = (sub & d) == 0
        for r in range(8):
            if r & d:
                continue
            lo, hi = vs[r], vs[r + d]
            vs[r] = jnp.where(keep, lo, pltpu.roll(hi, d, 0))
            vs[r + d] = jnp.where(keep, pltpu.roll(lo, 8 - d, 0), hi)
    return vs


INPROJ_SUB = 512


def _inproj_kernel(x_ref, g_ref, w_ref, q_ref, k_ref, v_ref, u_ref, uf_ref, ga_ref, gs_ref, u_sc, *, tm):
    for r0 in range(0, tm, INPROJ_SUB):
        rows = slice(r0, r0 + INPROJ_SUB)
        h = _rms(x_ref[0, rows, :], g_ref[...]).astype(BF16)

        def proj(c0, n):
            return jnp.dot(h, w_ref[:, c0:c0 + n], preferred_element_type=F32)

        q_ref[0, rows, :] = (proj(0, D_ATT) * (HEAD_DIM ** -0.5)).astype(BF16)
        k_ref[0, rows, :] = proj(D_ATT, D_ATT).astype(BF16)
        v_ref[0, rows, :] = proj(2 * D_ATT, D_ATT).astype(BF16)
        u = proj(3 * D_ATT, D_SSM)
        u_ref[0, rows, :] = u
        for jt in range(D_SSM // LANES):
            u_sc[jt, rows, :] = u[:, jt * LANES:(jt + 1) * LANES]
        ga_ref[0, rows, :] = proj(3 * D_ATT + D_SSM, D_MODEL).astype(BF16)
        gs_ref[0, rows, :] = proj(3 * D_ATT + D_SSM + D_MODEL, D_MODEL).astype(BF16)

        n_rb = INPROJ_SUB // (8 * CHUNK)
        pieces = [[[None, None] for _ in range(n_rb)] for _ in range(N_GROUPS)]
        for rb in range(n_rb):
            for jt in range(D_SSM // LANES):
                for half in range(2):
                    vs = [u_sc[jt, pl.ds(r0 + rb * 8 * CHUNK + 8 * half + sl, 8, stride=CHUNK), :]
                          for sl in range(8)]
                    out = _block_transpose8(vs)
                    for gl in range(8):
                        pieces[8 * jt + gl][rb][half] = out[gl]
        for g in range(N_GROUPS):
            blocks = [jnp.concatenate(pieces[g][rb], axis=1) for rb in range(n_rb)]
            uf_ref[0, g, r0 // CHUNK:(r0 + INPROJ_SUB) // CHUNK, :] = jnp.concatenate(blocks, axis=0).astype(BF16)


def _inproj(x, g_pre, w_in, *, tm=1024):
    B, L, _ = x.shape
    nc = L // CHUNK
    row = lambda n: pl.BlockSpec((1, tm, n), lambda b, i: (b, i, 0))
    const = lambda shape: pl.BlockSpec(shape, lambda b, i: (0,) * len(shape),
                                       pipeline_mode=pl.Buffered(1))
    return pl.pallas_call(
        functools.partial(_inproj_kernel, tm=tm),
        grid=(B, L // tm),
        in_specs=[row(D_MODEL), const((1, D_MODEL)), const((D_MODEL, D_IN))],
        out_specs=[row(D_ATT), row(D_ATT), row(D_ATT), row(D_SSM),
                   pl.BlockSpec((1, N_GROUPS, tm // CHUNK, FLAT), lambda b, i: (b, 0, i, 0)),
                   row(D_MODEL), row(D_MODEL)],
        out_shape=[jax.ShapeDtypeStruct((B, L, D_ATT), BF16)] * 3
        + [jax.ShapeDtypeStruct((B, L, D_SSM), F32),
           jax.ShapeDtypeStruct((B, N_GROUPS, nc, FLAT), BF16),
           jax.ShapeDtypeStruct((B, L, D_MODEL), BF16),
           jax.ShapeDtypeStruct((B, L, D_MODEL), BF16)],
        scratch_shapes=[pltpu.VMEM((D_SSM // LANES, tm, LANES), F32)],
        compiler_params=pltpu.CompilerParams(
            dimension_semantics=("arbitrary", "arbitrary"), vmem_limit_bytes=VMEM_LIMIT),
        name="inproj",
    )(x, g_pre, w_in)


N_DR = 2 * NA_KH - 1
_ROW_PATTERNS = ((0, (0, 0, 0, 0)), (4, (0, 1, 2, 3)), (8, (4, 4, 4, 4)))


def _attn_bias_blocks(rpb):
    qc = np.arange(GRID_W)[:, None]
    kc = np.arange(GRID_W)[None, :]
    col_start = np.clip(qc - NA_KW // 2, 0, GRID_W - NA_KW)
    col_ok = (kc >= col_start) & (kc < col_start + NA_KW)
    n_dc = 2 * NA_KW - 1
    pick = (np.arange(n_dc)[:, None, None] == (kc - qc + NA_KW - 1)[None]) & col_ok[None]
    tab = jnp.dot(rpb.reshape(N_HEADS * N_DR, n_dc), jnp.asarray(pick.reshape(n_dc, -1), F32),
                  precision=lax.Precision.HIGHEST).reshape(N_HEADS, N_DR, GRID_W, GRID_W)
    tab = jnp.where(col_ok[None, None], tab, NEG_BIG)
    tab = jnp.concatenate([tab, jnp.full((N_HEADS, 1, GRID_W, GRID_W), NEG_BIG, F32)], axis=1)
    return jnp.concatenate([tab, tab], axis=3).astype(F32)


def _assemble_bias(tab_ref, bias_sc):
    def per_head(h, carry):
        r0 = (h % 2) * (Q_ROWS * GRID_W)
        for p, (off, first_key) in enumerate(_ROW_PATTERNS):
            for i in range(Q_ROWS):
                for j in range(K_ROWS):
                    ok = first_key[i] <= j < first_key[i] + NA_KH
                    dr = j - i - off + NA_KH - 1 if ok else N_DR
                    half = slice((j % 2) * GRID_W, (j % 2 + 1) * GRID_W)
                    bias_sc[p, h // 2, pl.ds(pl.multiple_of(r0 + i * GRID_W, GRID_W), GRID_W),
                            j * GRID_W:(j + 1) * GRID_W] = tab_ref[h, dr, :, half]
        return carry
    lax.fori_loop(0, N_HEADS, per_head, 0)


def _attn_kernel(q_ref, k_ref, v_ref, tab_ref, o_ref, bias_ref, *, rows):
    n_groups = rows // Q_ROWS
    nq, nk = Q_ROWS * GRID_W, K_ROWS * GRID_W
    lane = lax.broadcasted_iota(jnp.int32, (nq, LANES), 1)

    @pl.when((pl.program_id(0) == 0) & (pl.program_id(1) == 0))
    def _():
        _assemble_bias(tab_ref, bias_ref)

    for gi in range(GROUPS_PER_STEP):
        g = pl.program_id(1) * GROUPS_PER_STEP + gi
        kr0 = jnp.clip(g * Q_ROWS - NA_KH // 2, 0, rows - K_ROWS)
        pat = jnp.where(g == 0, 0, jnp.where(g == n_groups - 1, 2, 1))
        start = pl.multiple_of(kr0 * GRID_W, Q_ROWS * GRID_W)
        qrows = slice(gi * nq, (gi + 1) * nq)
        for hp in range(N_HEADS // 2):
            cols = slice(hp * LANES, (hp + 1) * LANES)
            qp = q_ref[0, qrows, cols]
            kp = k_ref[0, pl.ds(start, nk), cols]
            vp = v_ref[0, pl.ds(start, nk), cols]
            q2 = jnp.concatenate([jnp.where(lane < HEAD_DIM, qp, jnp.zeros_like(qp)),
                                  jnp.where(lane >= HEAD_DIM, qp, jnp.zeros_like(qp))], axis=0)
            s = lax.dot_general(q2, kp, (((1,), (1,)), ((), ())), preferred_element_type=F32)
            s = s + bias_ref[pat, hp]
            m = jnp.max(s, axis=-1, keepdims=True)
            p = jnp.exp(s - m)
            l = jnp.sum(p, axis=-1, keepdims=True)
            o = jnp.dot(p.astype(BF16), vp, preferred_element_type=F32) / l
            o_ref[0, qrows, cols] = jnp.where(lane < HEAD_DIM, o[:nq], o[nq:]).astype(BF16)


def _attention(q, k, v, bias_blocks):
    B, L, _ = q.shape
    rows = L // GRID_W
    assert rows % (Q_ROWS * GROUPS_PER_STEP) == 0 and rows >= 3 * Q_ROWS
    nq = Q_ROWS * GRID_W * GROUPS_PER_STEP
    seq = pl.BlockSpec((1, L, D_ATT), lambda b, g: (b, 0, 0))
    return pl.pallas_call(
        functools.partial(_attn_kernel, rows=rows),
        grid=(B, rows // (Q_ROWS * GROUPS_PER_STEP)),
        in_specs=[pl.BlockSpec((1, nq, D_ATT), lambda b, g: (b, g, 0)), seq, seq,
                  pl.BlockSpec(bias_blocks.shape, lambda b, g: (0, 0, 0, 0), pipeline_mode=pl.Buffered(1))],
        out_specs=pl.BlockSpec((1, nq, D_ATT), lambda b, g: (b, g, 0)),
        out_shape=jax.ShapeDtypeStruct((B, L, D_ATT), BF16),
        scratch_shapes=[pltpu.VMEM((len(_ROW_PATTERNS), N_HEADS // 2, 2 * Q_ROWS * GRID_W, K_ROWS * GRID_W), F32)],
        compiler_params=pltpu.CompilerParams(
            dimension_semantics=("arbitrary", "arbitrary"), vmem_limit_bytes=ATTN_VMEM_LIMIT),
        name="attn",
    )(q, k, v, bias_blocks)


def _cmul(ar, ai, br, bi):
    return ar * br - ai * bi, ar * bi + ai * br


def _ssm_tables(lam_re, lam_im, log_dt, b_re, b_im, c_re, c_im):
    hp = lax.Precision.HIGHEST
    lam_re, lam_im = lam_re.astype(F32), lam_im.astype(F32)
    dt = jnp.exp(log_dt.astype(F32))[..., None]
    zr, zi = lam_re * dt, lam_im * dt
    d = jnp.arange(CHUNK + 1, dtype=F32)[:, None, None, None]
    mag = jnp.exp(zr[None] * d)
    pw_re, pw_im = mag * jnp.cos(zi[None] * d), mag * jnp.sin(zi[None] * d)
    nr, ni = pw_re[1] - 1.0, pw_im[1]
    den = lam_re * lam_re + lam_im * lam_im
    fr, fi = (nr * lam_re + ni * lam_im) / den, (ni * lam_re - nr * lam_im) / den
    bb_re, bb_im = _cmul(fr[..., None], fi[..., None], b_re.astype(F32), b_im.astype(F32))
    c_re, c_im = c_re.astype(F32), c_im.astype(F32)

    c_t_re, c_t_im = jnp.swapaxes(c_re, 2, 3)[..., None], jnp.swapaxes(c_im, 2, 3)[..., None]
    e_re, e_im = _cmul(c_t_re, c_t_im, bb_re[:, :, :, None, :], bb_im[:, :, :, None, :])
    e_re = e_re.reshape(2, N_GROUPS, STATE_P, SSM_GROUP * SSM_GROUP)
    e_im = e_im.reshape(2, N_GROUPS, STATE_P, SSM_GROUP * SSM_GROUP)
    kern = (jnp.einsum('dzgp,zgpn->dzgn', pw_re[:CHUNK], e_re, precision=hp)
            - jnp.einsum('dzgp,zgpn->dzgn', pw_im[:CHUNK], e_im, precision=hp))
    kf, kb = kern[:, 0], kern[:, 1]
    k_all = jnp.concatenate([kb[:0:-1], (kf[0] + kb[0])[None], kf[1:]], axis=0)
    k_t = jnp.transpose(k_all.reshape(2 * CHUNK - 1, N_GROUPS, SSM_GROUP, SSM_GROUP), (1, 3, 0, 2))
    k_t = k_t.reshape(N_GROUPS, SSM_GROUP, (2 * CHUNK - 1) * SSM_GROUP).astype(BF16)
    m = jnp.stack([k_t[:, :, (CHUNK - 1 - s) * SSM_GROUP:(CHUNK - 1 - s) * SSM_GROUP + FLAT]
                   for s in range(CHUNK)], axis=1)
    m = m.reshape(N_GROUPS, FLAT, FLAT)

    def state_in(z, powers):
        pr, pi = _cmul(pw_re[powers, z][:, :, :, None], pw_im[powers, z][:, :, :, None],
                       bb_re[z][None], bb_im[z][None])
        both = jnp.concatenate([pr, pi], axis=2).astype(BF16)
        return jnp.transpose(both, (1, 0, 3, 2)).reshape(N_GROUPS, FLAT, 2 * STATE_P)

    def state_out(z, powers):
        qr, qi = _cmul(c_re[z][None], c_im[z][None],
                       pw_re[powers, z][:, :, None, :], pw_im[powers, z][:, :, None, :])
        both = jnp.concatenate([qr, -qi], axis=3).astype(BF16)
        return jnp.transpose(both, (1, 3, 0, 2)).reshape(N_GROUPS, 2 * STATE_P, FLAT)

    fwd = np.arange(CHUNK)
    pp = jnp.concatenate([state_in(0, CHUNK - 1 - fwd), state_in(1, fwd)], axis=2)
    sq = jnp.concatenate([state_out(0, fwd + 1), state_out(1, CHUNK - fwd)], axis=1)
    ar, ai = pw_re[CHUNK], pw_im[CHUNK]
    ca = jnp.concatenate([ar, ar], axis=-1)
    cb = jnp.concatenate([-ai, ai], axis=-1)
    coef = jnp.stack([ca[0], cb[0], ca[1], cb[1]])
    return pp, m, sq, coef


def _ssm_kernel(uf_ref, pp_ref, m_ref, sq_ref, coef_ref, y_ref, s_ref, so_ref, *, nc):
    half = 2 * STATE_P
    for gb in range(N_GROUPS // 8):
        ds = [jnp.dot(uf_ref[0, 8 * gb + gl], pp_ref[8 * gb + gl], preferred_element_type=F32)
              for gl in range(8)]
        for z in range(2):
            for cb in range(nc // 8):
                out = _sublane_transpose8([d[cb * 8:(cb + 1) * 8, z * half:(z + 1) * half] for d in ds])
                for cl in range(8):
                    r0 = (cb * 8 + cl) * N_GROUPS + 8 * gb
                    s_ref[z, r0:r0 + 8, :] = out[cl]

    caf, cbf, cab, cbb = coef_ref[0], coef_ref[1], coef_ref[2], coef_ref[3]

    def step(c, carry):
        sf, sfx, sb, sbx = carry
        rf = pl.multiple_of(c * N_GROUPS, N_GROUPS)
        rb = pl.multiple_of((nc - 1 - c) * N_GROUPS, N_GROUPS)
        df = s_ref[0, pl.ds(rf, N_GROUPS), :]
        db = s_ref[1, pl.ds(rb, N_GROUPS), :]
        so_ref[0, pl.ds(rf, N_GROUPS), :] = sf
        so_ref[1, pl.ds(rb, N_GROUPS), :] = sb
        dfx = pltpu.roll(df, STATE_P, 1)
        dbx = pltpu.roll(db, STATE_P, 1)
        sf, sfx = caf * sf + cbf * sfx + df, caf * sfx - cbf * sf + dfx
        sb, sbx = cab * sb + cbb * sbx + db, cab * sbx - cbb * sb + dbx
        return sf, sfx, sb, sbx

    zero = jnp.zeros((N_GROUPS, half), F32)
    lax.fori_loop(0, nc, step, (zero, zero, zero, zero), unroll=SCAN_UNROLL)

    for gb in range(N_GROUPS // 8):
        states = [[[], []] for _ in range(8)]
        for z in range(2):
            for cb in range(nc // 8):
                r0s = [(cb * 8 + cl) * N_GROUPS + 8 * gb for cl in range(8)]
                out = _sublane_transpose8([so_ref[z, r0:r0 + 8, :] for r0 in r0s])
                for gl in range(8):
                    states[gl][z].append(out[gl])
        for gl in range(8):
            g = 8 * gb + gl
            sf = jnp.concatenate(states[gl][0], axis=0).astype(BF16)
            sb = jnp.concatenate(states[gl][1], axis=0).astype(BF16)
            y_ref[0, g] = (jnp.dot(uf_ref[0, g], m_ref[g], preferred_element_type=F32)
                           + jnp.dot(jnp.concatenate([sf, sb], axis=1), sq_ref[g], preferred_element_type=F32))


def _ssm(uf, pp, m, sq, coef):
    B, _, nc, _ = uf.shape
    const = lambda a: pl.BlockSpec(a.shape, lambda b: (0,) * a.ndim, pipeline_mode=pl.Buffered(1))
    blk = pl.BlockSpec((1, N_GROUPS, nc, FLAT), lambda b: (b, 0, 0, 0))
    return pl.pallas_call(
        functools.partial(_ssm_kernel, nc=nc),
        grid=(B,),
        in_specs=[blk, const(pp), const(m), const(sq), const(coef)],
        out_specs=blk,
        out_shape=jax.ShapeDtypeStruct((B, N_GROUPS, nc, FLAT), F32),
        scratch_shapes=[pltpu.VMEM((2, nc * N_GROUPS, 2 * STATE_P), F32)] * 2,
        compiler_params=pltpu.CompilerParams(
            dimension_semantics=("arbitrary",), vmem_limit_bytes=VMEM_LIMIT),
        name="ssm",
    )(uf, pp, m, sq, coef)


MERGE_SUB = 512


def _merge_kernel(x_ref, att_ref, yf_ref, u_ref, ga_ref, gs_ref, d_ref, wglu_ref, bglu_ref,
                  watt_ref, wssm_ref, wout_ref, gpost_ref, o_ref, y_sc, *, tm):
    for rb in range(tm // (8 * CHUNK)):
        for jt in range(D_SSM // LANES):
            for half in range(2):
                vs = [yf_ref[0, 8 * jt + gl, rb * 8:(rb + 1) * 8, half * LANES:(half + 1) * LANES]
                      for gl in range(8)]
                out = _block_transpose8(vs)
                for sl in range(8):
                    y_sc[jt, pl.ds(rb * 8 * CHUNK + 8 * half + sl, 8, stride=CHUNK), :] = out[sl]

    for r0 in range(0, tm, MERGE_SUB):
        rows = slice(r0, r0 + MERGE_SUB)
        y = jnp.concatenate([y_sc[jt, rows, :] for jt in range(D_SSM // LANES)], axis=1)
        y = _gelu(y + d_ref[...] * u_ref[0, rows, :])
        z = jnp.dot(y.astype(BF16), wglu_ref[...], preferred_element_type=F32) + bglu_ref[...]
        ssm = (y * jax.nn.sigmoid(z)).astype(BF16)
        a = jnp.dot(att_ref[0, rows, :], watt_ref[...], preferred_element_type=F32)
        b = jnp.dot(ssm, wssm_ref[...], preferred_element_type=F32)
        merged = (jax.nn.sigmoid(ga_ref[0, rows, :].astype(F32)) * a
                  + jax.nn.sigmoid(gs_ref[0, rows, :].astype(F32)) * b)
        mo = jnp.dot(merged.astype(BF16), wout_ref[...], preferred_element_type=F32)
        o_ref[0, rows, :] = x_ref[0, rows, :] + _rms(mo, gpost_ref[...])


def _merge(x, att, yf, u, ga, gs, d_skip, w_glu, b_glu, w_att, w_ssm, w_out, g_post, *, tm=1024):
    B, L, _ = x.shape
    row = lambda n: pl.BlockSpec((1, tm, n), lambda b, i: (b, i, 0))
    const = lambda a: pl.BlockSpec(a.shape, lambda b, i: (0,) * a.ndim, pipeline_mode=pl.Buffered(1))
    consts = (d_skip, w_glu, b_glu, w_att, w_ssm, w_out, g_post)
    return pl.pallas_call(
        functools.partial(_merge_kernel, tm=tm),
        grid=(B, L // tm),
        in_specs=[row(D_MODEL), row(D_ATT),
                  pl.BlockSpec((1, N_GROUPS, tm // CHUNK, FLAT), lambda b, i: (b, 0, i, 0)),
                  row(D_SSM), row(D_MODEL), row(D_MODEL)] + [const(a) for a in consts],
        out_specs=row(D_MODEL),
        out_shape=jax.ShapeDtypeStruct((B, L, D_MODEL), F32),
        scratch_shapes=[pltpu.VMEM((D_SSM // LANES, tm, LANES), F32)],
        compiler_params=pltpu.CompilerParams(
            dimension_semantics=("arbitrary", "arbitrary"), vmem_limit_bytes=VMEM_LIMIT),
        name="merge",
    )(x, att, yf, u, ga, gs, *consts)


HALO = 8
FF_CHUNK = 256


def _ffn_kernel(xp_ref, x_ref, xn_ref, gpre_ref, wup_ref, cw_ref, cb_ref, wdn_ref, gpost_ref, o_ref,
                *, tm, n_tiles):
    i = pl.program_id(1)
    x = x_ref[0]
    xp = jnp.where(i == 0, 0.0, xp_ref[0])
    xn = jnp.where(i == n_tiles - 1, 0.0, xn_ref[0])
    h = _rms(jnp.concatenate([xp, x, xn], axis=0), gpre_ref[...]).astype(BF16)
    n_ext = tm + 2 * HALO

    def conv(c0):
        up = jnp.dot(h, wup_ref[:, c0:c0 + FF_CHUNK], preferred_element_type=F32)
        w = cw_ref[:, c0:c0 + FF_CHUNK]
        prev = pltpu.roll(up, 1, 0)
        nxt = pltpu.roll(up, n_ext - 1, 0)
        r = prev * w[0:1] + up * w[1:2] + nxt * w[2:3] + cb_ref[:, c0:c0 + FF_CHUNK]
        return r[HALO:HALO + tm]

    acts = [(_gelu(conv(c * FF_CHUNK)) * conv(D_FF + c * FF_CHUNK)).astype(BF16)
            for c in range(D_FF // FF_CHUNK)]
    acc = jnp.dot(jnp.concatenate(acts, axis=1), wdn_ref[...], preferred_element_type=F32)
    o_ref[0] = x + _rms(acc, gpost_ref[...])


def _ffn(x, g_pre, w_up, conv_w, conv_b, w_down, g_post, *, tm=1024):
    B, L, _ = x.shape
    n_tiles = L // tm
    hb = tm // HALO
    const = lambda a: pl.BlockSpec(a.shape, lambda b, i: (0,) * a.ndim, pipeline_mode=pl.Buffered(1))
    consts_a = (g_pre, w_up, conv_w, conv_b, w_down, g_post)
    return pl.pallas_call(
        functools.partial(_ffn_kernel, tm=tm, n_tiles=n_tiles),
        grid=(B, n_tiles),
        in_specs=[pl.BlockSpec((1, HALO, D_MODEL), lambda b, i: (b, jnp.maximum(i * hb - 1, 0), 0)),
                  pl.BlockSpec((1, tm, D_MODEL), lambda b, i: (b, i, 0)),
                  pl.BlockSpec((1, HALO, D_MODEL),
                               lambda b, i: (b, jnp.minimum((i + 1) * hb, L // HALO - 1), 0))]
        + [const(a) for a in consts_a],
        out_specs=pl.BlockSpec((1, tm, D_MODEL), lambda b, i: (b, i, 0)),
        out_shape=jax.ShapeDtypeStruct((B, L, D_MODEL), F32),
        compiler_params=pltpu.CompilerParams(
            dimension_semantics=("arbitrary", "arbitrary"), vmem_limit_bytes=VMEM_LIMIT),
        name="ffn",
    )(x, x, x, *consts_a)


def _layer(x, p):
    q, k, v, u, uf, ga, gs = _inproj(x, p['g_mix_pre'], p['w_in'])
    att = _attention(q, k, v, p['attn_bias'])
    yf = _ssm(uf, p['pp'], p['m'], p['sq'], p['coef'])
    x1 = _merge(x, att, yf, u, ga, gs, p['ssm_d'], p['w_glu'], p['b_glu'], p['w_branch_att'],
                p['w_branch_ssm'], p['w_out'], p['g_mix_post'])
    return _ffn(x1, p['g_ffn_pre'], p['w_up'], p['conv_w'], p['conv_b'], p['w_down'], p['g_ffn_post'])


def _prepare(l, g_mix_pre, g_mix_post, w_in, attn_rpb, ssm_lam_re, ssm_lam_im, ssm_log_dt, ssm_b_re,
             ssm_b_im, ssm_c_re, ssm_c_im, ssm_d, w_glu, b_glu, w_branch_att, w_branch_ssm, w_out,
             g_ffn_pre, g_ffn_post, w_up, conv_w, conv_b, w_down):
    pp, m, sq, coef = _ssm_tables(ssm_lam_re[l], ssm_lam_im[l], ssm_log_dt[l], ssm_b_re[l], ssm_b_im[l],
                               ssm_c_re[l], ssm_c_im[l])
    vec = lambda a: a[l].astype(F32).reshape(1, -1)
    return dict(
        g_mix_pre=vec(g_mix_pre), g_mix_post=vec(g_mix_post), w_in=w_in[l].astype(BF16),
        attn_bias=_attn_bias_blocks(attn_rpb[l].astype(F32)), pp=pp, m=m, sq=sq, coef=coef, ssm_d=vec(ssm_d),
        w_glu=w_glu[l].astype(BF16), b_glu=vec(b_glu), w_branch_att=w_branch_att[l].astype(BF16),
        w_branch_ssm=w_branch_ssm[l].astype(BF16), w_out=w_out[l].astype(BF16),
        g_ffn_pre=vec(g_ffn_pre), g_ffn_post=vec(g_ffn_post), w_up=w_up[l].astype(BF16),
        conv_w=conv_w[l].astype(F32), conv_b=vec(conv_b), w_down=w_down[l].astype(BF16))


def kernel(x_prompt, x_sample, g_mix_pre, g_mix_post, w_in, attn_rpb, ssm_lam_re, ssm_lam_im, ssm_log_dt,
           ssm_b_re, ssm_b_im, ssm_c_re, ssm_c_im, ssm_d, w_glu, b_glu, w_branch_att, w_branch_ssm, w_out,
           g_ffn_pre, g_ffn_post, w_up, conv_w, conv_b, w_down):
    weights = (g_mix_pre, g_mix_post, w_in, attn_rpb, ssm_lam_re, ssm_lam_im, ssm_log_dt, ssm_b_re,
               ssm_b_im, ssm_c_re, ssm_c_im, ssm_d, w_glu, b_glu, w_branch_att, w_branch_ssm, w_out,
               g_ffn_pre, g_ffn_post, w_up, conv_w, conv_b, w_down)
    layers = [_prepare(l, *weights) for l in range(w_in.shape[0])]

    def trunk(x):
        for p in layers:
            x = _layer(x, p)
        return x

    return trunk(x_prompt), trunk(x_sample)
```

```python
import functools

import numpy as np
import jax
import jax.numpy as jnp
from jax import lax
from jax.experimental import pallas as pl
from jax.experimental.pallas import tpu as pltpu

D_MODEL = 1024
GRID_W = 64
N_HEADS = 8
HEAD_DIM = 64
D_ATT = N_HEADS * HEAD_DIM
NA_KH = 8
NA_KW = 16
SSM_GROUP = 16
D_SSM = 512
N_GROUPS = D_SSM // SSM_GROUP
STATE_P = 64
D_FF = 2816
D_IN = 3 * D_ATT + D_SSM + 2 * D_MODEL
EPS = 1e-6
NEG_BIG = -1e30

CHUNK = 16
FLAT = CHUNK * SSM_GROUP
SCAN_UNROLL = 32
Q_ROWS = 4
K_ROWS = 12
GROUPS_PER_STEP = 4
LANES = 128
VMEM_LIMIT = 56 * 1024 * 1024
ATTN_VMEM_LIMIT = 62 * 1024 * 1024
FFN_VMEM_LIMIT = 60 * 1024 * 1024

F32 = jnp.float32
BF16 = jnp.bfloat16


def _rms(x, g):
    return x * lax.rsqrt(jnp.mean(x * x, axis=-1, keepdims=True) + EPS) * g


_GELU_C0 = float(np.sqrt(2.0 / np.pi))
_GELU_C1 = _GELU_C0 * 0.044715


def _gelu(x):
    return (0.5 * x) * (1.0 + jnp.tanh(x * (_GELU_C0 + _GELU_C1 * (x * x))))


def _block_transpose8(vs):
    lane = lax.broadcasted_iota(jnp.int32, vs[0].shape, 1)
    vs = list(vs)
    for d in (4, 2, 1):
        keep = (lane & (d * SSM_GROUP)) == 0
        for r in range(8):
            if r & d:
                continue
            lo, hi = vs[r], vs[r + d]
            vs[r] = jnp.where(keep, lo, pltpu.roll(hi, d * SSM_GROUP, 1))
            vs[r + d] = jnp.where(keep, pltpu.roll(lo, LANES - d * SSM_GROUP, 1), hi)
    return vs


def _sublane_transpose8(vs):
    sub = lax.broadcasted_iota(jnp.int32, vs[0].shape, 0)
    vs = list(vs)
    for d in (4, 2, 1):
        keep = (sub & d) == 0
        for r in range(8):
            if r & d:
                continue
            lo, hi = vs[r], vs[r + d]
            vs[r] = jnp.where(keep, lo, pltpu.roll(hi, d, 0))
            vs[r + d] = jnp.where(keep, pltpu.roll(lo, 8 - d, 0), hi)
    return vs


INPROJ_SUB = 512


def _inproj_kernel(x_ref, g_ref, w_ref, q_ref, k_ref, v_ref, u_ref, uf_ref, ga_ref, gs_ref, u_sc, *, tm):
    for r0 in range(0, tm, INPROJ_SUB):
        rows = slice(r0, r0 + INPROJ_SUB)
        h = _rms(x_ref[0, rows, :], g_ref[...]).astype(BF16)

        def proj(c0, n):
            return jnp.dot(h, w_ref[:, c0:c0 + n], preferred_element_type=F32)

        q_ref[0, rows, :] = (proj(0, D_ATT) * (HEAD_DIM ** -0.5)).astype(BF16)
        k_ref[0, rows, :] = proj(D_ATT, D_ATT).astype(BF16)
        v_ref[0, rows, :] = proj(2 * D_ATT, D_ATT).astype(BF16)
        u = proj(3 * D_ATT, D_SSM)
        u_ref[0, rows, :] = u
        for jt in range(D_SSM // LANES):
            u_sc[jt, rows, :] = u[:, jt * LANES:(jt + 1) * LANES]
        ga_ref[0, rows, :] = proj(3 * D_ATT + D_SSM, D_MODEL).astype(BF16)
        gs_ref[0, rows, :] = proj(3 * D_ATT + D_SSM + D_MODEL, D_MODEL).astype(BF16)

        n_rb = INPROJ_SUB // (8 * CHUNK)
        pieces = [[[None, None] for _ in range(n_rb)] for _ in range(N_GROUPS)]
        for rb in range(n_rb):
            for jt in range(D_SSM // LANES):
                for half in range(2):
                    vs = [u_sc[jt, pl.ds(r0 + rb * 8 * CHUNK + 8 * half + sl, 8, stride=CHUNK), :]
                          for sl in range(8)]
                    out = _block_transpose8(vs)
                    for gl in range(8):
                        pieces[8 * jt + gl][rb][half] = out[gl]
        for g in range(N_GROUPS):
            blocks = [jnp.concatenate(pieces[g][rb], axis=1) for rb in range(n_rb)]
            uf_ref[0, g, r0 // CHUNK:(r0 + INPROJ_SUB) // CHUNK, :] = jnp.concatenate(blocks, axis=0).astype(BF16)


def _inproj(x, g_pre, w_in, *, tm=1024):
    B, L, _ = x.shape
    nc = L // CHUNK
    row = lambda n: pl.BlockSpec((1, tm, n), lambda b, i: (b, i, 0))
    const = lambda shape: pl.BlockSpec(shape, lambda b, i: (0,) * len(shape),
                                       pipeline_mode=pl.Buffered(1))
    return pl.pallas_call(
        functools.partial(_inproj_kernel, tm=tm),
        grid=(B, L // tm),
        in_specs=[row(D_MODEL), const((1, D_MODEL)), const((D_MODEL, D_IN))],
        out_specs=[row(D_ATT), row(D_ATT), row(D_ATT), row(D_SSM),
                   pl.BlockSpec((1, N_GROUPS, tm // CHUNK, FLAT), lambda b, i: (b, 0, i, 0)),
                   row(D_MODEL), row(D_MODEL)],
        out_shape=[jax.ShapeDtypeStruct((B, L, D_ATT), BF16)] * 3
        + [jax.ShapeDtypeStruct((B, L, D_SSM), F32),
           jax.ShapeDtypeStruct((B, N_GROUPS, nc, FLAT), BF16),
           jax.ShapeDtypeStruct((B, L, D_MODEL), BF16),
           jax.ShapeDtypeStruct((B, L, D_MODEL), BF16)],
        scratch_shapes=[pltpu.VMEM((D_SSM // LANES, tm, LANES), F32)],
        compiler_params=pltpu.CompilerParams(
            dimension_semantics=("arbitrary", "arbitrary"), vmem_limit_bytes=VMEM_LIMIT),
        name="inproj",
    )(x, g_pre, w_in)


N_DR = 2 * NA_KH - 1
_ROW_PATTERNS = ((0, (0, 0, 0, 0)), (4, (0, 1, 2, 3)), (8, (4, 4, 4, 4)))


def _attn_bias_blocks(rpb):
    qc = np.arange(GRID_W)[:, None]
    kc = np.arange(GRID_W)[None, :]
    col_start = np.clip(qc - NA_KW // 2, 0, GRID_W - NA_KW)
    col_ok = (kc >= col_start) & (kc < col_start + NA_KW)
    n_dc = 2 * NA_KW - 1
    pick = (np.arange(n_dc)[:, None, None] == (kc - qc + NA_KW - 1)[None]) & col_ok[None]
    tab = jnp.dot(rpb.reshape(N_HEADS * N_DR, n_dc), jnp.asarray(pick.reshape(n_dc, -1), F32),
                  precision=lax.Precision.HIGHEST).reshape(N_HEADS, N_DR, GRID_W, GRID_W)
    tab = jnp.where(col_ok[None, None], tab, NEG_BIG)
    tab = jnp.concatenate([tab, jnp.full((N_HEADS, 1, GRID_W, GRID_W), NEG_BIG, F32)], axis=1)
    return jnp.concatenate([tab, tab], axis=3).astype(F32)


def _assemble_bias(tab_ref, bias_sc):
    def per_head(h, carry):
        r0 = (h % 2) * (Q_ROWS * GRID_W)
        for p, (off, first_key) in enumerate(_ROW_PATTERNS):
            for i in range(Q_ROWS):
                for j in range(K_ROWS):
                    ok = first_key[i] <= j < first_key[i] + NA_KH
                    dr = j - i - off + NA_KH - 1 if ok else N_DR
                    half = slice((j % 2) * GRID_W, (j % 2 + 1) * GRID_W)
                    bias_sc[p, h // 2, pl.ds(pl.multiple_of(r0 + i * GRID_W, GRID_W), GRID_W),
                            j * GRID_W:(j + 1) * GRID_W] = tab_ref[h, dr, :, half]
        return carry
    lax.fori_loop(0, N_HEADS, per_head, 0)


def _attn_kernel(q_ref, k_ref, v_ref, tab_ref, o_ref, bias_ref, *, rows):
    n_groups = rows // Q_ROWS
    nq, nk = Q_ROWS * GRID_W, K_ROWS * GRID_W
    lane = lax.broadcasted_iota(jnp.int32, (nq, LANES), 1)

    @pl.when((pl.program_id(0) == 0) & (pl.program_id(1) == 0))
    def _():
        _assemble_bias(tab_ref, bias_ref)

    for gi in range(GROUPS_PER_STEP):
        g = pl.program_id(1) * GROUPS_PER_STEP + gi
        kr0 = jnp.clip(g * Q_ROWS - NA_KH // 2, 0, rows - K_ROWS)
        pat = jnp.where(g == 0, 0, jnp.where(g == n_groups - 1, 2, 1))
        start = pl.multiple_of(kr0 * GRID_W, Q_ROWS * GRID_W)
        qrows = slice(gi * nq, (gi + 1) * nq)
        for hp in range(N_HEADS // 2):
            cols = slice(hp * LANES, (hp + 1) * LANES)
            qp = q_ref[0, qrows, cols]
            kp = k_ref[0, pl.ds(start, nk), cols]
            vp = v_ref[0, pl.ds(start, nk), cols]
            q2 = jnp.concatenate([jnp.where(lane < HEAD_DIM, qp, jnp.zeros_like(qp)),
                                  jnp.where(lane >= HEAD_DIM, qp, jnp.zeros_like(qp))], axis=0)
            s = lax.dot_general(q2, kp, (((1,), (1,)), ((), ())), preferred_element_type=F32)
            s = s + bias_ref[pat, hp]
            m = jnp.max(s, axis=-1, keepdims=True)
            p = jnp.exp(s - m)
            l = jnp.sum(p, axis=-1, keepdims=True)
            o = jnp.dot(p.astype(BF16), vp, preferred_element_type=F32) / l
            o_ref[0, qrows, cols] = jnp.where(lane < HEAD_DIM, o[:nq], o[nq:]).astype(BF16)


def _attention(q, k, v, bias_blocks):
    B, L, _ = q.shape
    rows = L // GRID_W
    assert rows % (Q_ROWS * GROUPS_PER_STEP) == 0 and rows >= 3 * Q_ROWS
    nq = Q_ROWS * GRID_W * GROUPS_PER_STEP
    seq = pl.BlockSpec((1, L, D_ATT), lambda b, g: (b, 0, 0))
    return pl.pallas_call(
        functools.partial(_attn_kernel, rows=rows),
        grid=(B, rows // (Q_ROWS * GROUPS_PER_STEP)),
        in_specs=[pl.BlockSpec((1, nq, D_ATT), lambda b, g: (b, g, 0)), seq, seq,
                  pl.BlockSpec(bias_blocks.shape, lambda b, g: (0, 0, 0, 0), pipeline_mode=pl.Buffered(1))],
        out_specs=pl.BlockSpec((1, nq, D_ATT), lambda b, g: (b, g, 0)),
        out_shape=jax.ShapeDtypeStruct((B, L, D_ATT), BF16),
        scratch_shapes=[pltpu.VMEM((len(_ROW_PATTERNS), N_HEADS // 2, 2 * Q_ROWS * GRID_W, K_ROWS * GRID_W), F32)],
        compiler_params=pltpu.CompilerParams(
            dimension_semantics=("arbitrary", "arbitrary"), vmem_limit_bytes=ATTN_VMEM_LIMIT),
        name="attn",
    )(q, k, v, bias_blocks)


def _cmul(ar, ai, br, bi):
    return ar * br - ai * bi, ar * bi + ai * br


def _ssm_tables(lam_re, lam_im, log_dt, b_re, b_im, c_re, c_im):
    hp = lax.Precision.HIGHEST
    lam_re, lam_im = lam_re.astype(F32), lam_im.astype(F32)
    dt = jnp.exp(log_dt.astype(F32))[..., None]
    zr, zi = lam_re * dt, lam_im * dt
    d = jnp.arange(CHUNK + 1, dtype=F32)[:, None, None, None]
    mag = jnp.exp(zr[None] * d)
    pw_re, pw_im = mag * jnp.cos(zi[None] * d), mag * jnp.sin(zi[None] * d)
    nr, ni = pw_re[1] - 1.0, pw_im[1]
    den = lam_re * lam_re + lam_im * lam_im
    fr, fi = (nr * lam_re + ni * lam_im) / den, (ni * lam_re - nr * lam_im) / den
    bb_re, bb_im = _cmul(fr[..., None], fi[..., None], b_re.astype(F32), b_im.astype(F32))
    c_re, c_im = c_re.astype(F32), c_im.astype(F32)

    c_t_re, c_t_im = jnp.swapaxes(c_re, 2, 3)[..., None], jnp.swapaxes(c_im, 2, 3)[..., None]
    e_re, e_im = _cmul(c_t_re, c_t_im, bb_re[:, :, :, None, :], bb_im[:, :, :, None, :])
    e_re = e_re.reshape(2, N_GROUPS, STATE_P, SSM_GROUP * SSM_GROUP)
    e_im = e_im.reshape(2, N_GROUPS, STATE_P, SSM_GROUP * SSM_GROUP)
    kern = (jnp.einsum('dzgp,zgpn->dzgn', pw_re[:CHUNK], e_re, precision=hp)
            - jnp.einsum('dzgp,zgpn->dzgn', pw_im[:CHUNK], e_im, precision=hp))
    kf, kb = kern[:, 0], kern[:, 1]
    k_all = jnp.concatenate([kb[:0:-1], (kf[0] + kb[0])[None], kf[1:]], axis=0)
    k_t = jnp.transpose(k_all.reshape(2 * CHUNK - 1, N_GROUPS, SSM_GROUP, SSM_GROUP), (1, 3, 0, 2))
    k_t = k_t.reshape(N_GROUPS, SSM_GROUP, (2 * CHUNK - 1) * SSM_GROUP).astype(BF16)
    m = jnp.stack([k_t[:, :, (CHUNK - 1 - s) * SSM_GROUP:(CHUNK - 1 - s) * SSM_GROUP + FLAT]
                   for s in range(CHUNK)], axis=1)
    m = m.reshape(N_GROUPS, FLAT, FLAT)

    def state_in(z, powers):
        pr, pi = _cmul(pw_re[powers, z][:, :, :, None], pw_im[powers, z][:, :, :, None],
                       bb_re[z][None], bb_im[z][None])
        both = jnp.concatenate([pr, pi], axis=2).astype(BF16)
        return jnp.transpose(both, (1, 0, 3, 2)).reshape(N_GROUPS, FLAT, 2 * STATE_P)

    def state_out(z, powers):
        qr, qi = _cmul(c_re[z][None], c_im[z][None],
                       pw_re[powers, z][:, :, None, :], pw_im[powers, z][:, :, None, :])
        both = jnp.concatenate([qr, -qi], axis=3).astype(BF16)
        return jnp.transpose(both, (1, 3, 0, 2)).reshape(N_GROUPS, 2 * STATE_P, FLAT)

    fwd = np.arange(CHUNK)
    pp = jnp.concatenate([state_in(0, CHUNK - 1 - fwd), state_in(1, fwd)], axis=2)
    sq = jnp.concatenate([state_out(0, fwd + 1), state_out(1, CHUNK - fwd)], axis=1)
    ar, ai = pw_re[CHUNK], pw_im[CHUNK]
    ca = jnp.concatenate([ar, ar], axis=-1)
    cb = jnp.concatenate([-ai, ai], axis=-1)
    coef = jnp.stack([ca[0], cb[0], ca[1], cb[1]])
    return pp, m, sq, coef


def _ssm_kernel(uf_ref, pp_ref, m_ref, sq_ref, coef_ref, y_ref, s_ref, so_ref, *, nc):
    half = 2 * STATE_P
    for gb in range(N_GROUPS // 8):
        ds = [jnp.dot(uf_ref[0, 8 * gb + gl], pp_ref[8 * gb + gl], preferred_element_type=F32)
              for gl in range(8)]
        for z in range(2):
            for cb in range(nc // 8):
                out = _sublane_transpose8([d[cb * 8:(cb + 1) * 8, z * half:(z + 1) * half] for d in ds])
                for cl in range(8):
                    r0 = (cb * 8 + cl) * N_GROUPS + 8 * gb
                    s_ref[z, r0:r0 + 8, :] = out[cl]

    caf, cbf, cab, cbb = coef_ref[0], coef_ref[1], coef_ref[2], coef_ref[3]

    def step(c, carry):
        sf, sfx, sb, sbx = carry
        rf = pl.multiple_of(c * N_GROUPS, N_GROUPS)
        rb = pl.multiple_of((nc - 1 - c) * N_GROUPS, N_GROUPS)
        df = s_ref[0, pl.ds(rf, N_GROUPS), :]
        db = s_ref[1, pl.ds(rb, N_GROUPS), :]
        so_ref[0, pl.ds(rf, N_GROUPS), :] = sf
        so_ref[1, pl.ds(rb, N_GROUPS), :] = sb
        dfx = pltpu.roll(df, STATE_P, 1)
        dbx = pltpu.roll(db, STATE_P, 1)
        sf, sfx = caf * sf + cbf * sfx + df, caf * sfx - cbf * sf + dfx
        sb, sbx = cab * sb + cbb * sbx + db, cab * sbx - cbb * sb + dbx
        return sf, sfx, sb, sbx

    zero = jnp.zeros((N_GROUPS, half), F32)
    lax.fori_loop(0, nc, step, (zero, zero, zero, zero), unroll=SCAN_UNROLL)

    for gb in range(N_GROUPS // 8):
        states = [[[], []] for _ in range(8)]
        for z in range(2):
            for cb in range(nc // 8):
                r0s = [(cb * 8 + cl) * N_GROUPS + 8 * gb for cl in range(8)]
                out = _sublane_transpose8([so_ref[z, r0:r0 + 8, :] for r0 in r0s])
                for gl in range(8):
                    states[gl][z].append(out[gl])
        for gl in range(8):
            g = 8 * gb + gl
            sf = jnp.concatenate(states[gl][0], axis=0).astype(BF16)
            sb = jnp.concatenate(states[gl][1], axis=0).astype(BF16)
            y_ref[0, g] = (jnp.dot(uf_ref[0, g], m_ref[g], preferred_element_type=F32)
                           + jnp.dot(jnp.concatenate([sf, sb], axis=1), sq_ref[g], preferred_element_type=F32))


def _ssm(uf, pp, m, sq, coef):
    B, _, nc, _ = uf.shape
    const = lambda a: pl.BlockSpec(a.shape, lambda b: (0,) * a.ndim, pipeline_mode=pl.Buffered(1))
    blk = pl.BlockSpec((1, N_GROUPS, nc, FLAT), lambda b: (b, 0, 0, 0))
    return pl.pallas_call(
        functools.partial(_ssm_kernel, nc=nc),
        grid=(B,),
        in_specs=[blk, const(pp), const(m), const(sq), const(coef)],
        out_specs=blk,
        out_shape=jax.ShapeDtypeStruct((B, N_GROUPS, nc, FLAT), F32),
        scratch_shapes=[pltpu.VMEM((2, nc * N_GROUPS, 2 * STATE_P), F32)] * 2,
        compiler_params=pltpu.CompilerParams(
            dimension_semantics=("arbitrary",), vmem_limit_bytes=VMEM_LIMIT),
        name="ssm",
    )(uf, pp, m, sq, coef)


MERGE_SUB = 512


def _merge_kernel(att_ref, yf_ref, u_ref, ga_ref, gs_ref, d_ref, wglu_ref, bglu_ref,
                  watt_ref, wssm_ref, wout_ref, gpost_ref, o_ref, y_sc, *, tm):
    for rb in range(tm // (8 * CHUNK)):
        for jt in range(D_SSM // LANES):
            for half in range(2):
                vs = [yf_ref[0, 8 * jt + gl, rb * 8:(rb + 1) * 8, half * LANES:(half + 1) * LANES]
                      for gl in range(8)]
                out = _block_transpose8(vs)
                for sl in range(8):
                    y_sc[jt, pl.ds(rb * 8 * CHUNK + 8 * half + sl, 8, stride=CHUNK), :] = out[sl]

    for r0 in range(0, tm, MERGE_SUB):
        rows = slice(r0, r0 + MERGE_SUB)
        y = jnp.concatenate([y_sc[jt, rows, :] for jt in range(D_SSM // LANES)], axis=1)
        y = _gelu(y + d_ref[...] * u_ref[0, rows, :])
        z = jnp.dot(y.astype(BF16), wglu_ref[...], preferred_element_type=F32) + bglu_ref[...]
        ssm = (y * jax.nn.sigmoid(z)).astype(BF16)
        a = jnp.dot(att_ref[0, rows, :], watt_ref[...], preferred_element_type=F32)
        b = jnp.dot(ssm, wssm_ref[...], preferred_element_type=F32)
        merged = (jax.nn.sigmoid(ga_ref[0, rows, :].astype(F32)) * a
                  + jax.nn.sigmoid(gs_ref[0, rows, :].astype(F32)) * b)
        mo = jnp.dot(merged.astype(BF16), wout_ref[...], preferred_element_type=F32)
        o_ref[0, rows, :] = _rms(mo, gpost_ref[...])


def _merge(att, yf, u, ga, gs, d_skip, w_glu, b_glu, w_att, w_ssm, w_out, g_post, *, tm=1024):
    B, L, _ = att.shape
    row = lambda n: pl.BlockSpec((1, tm, n), lambda b, i: (b, i, 0))
    const = lambda a: pl.BlockSpec(a.shape, lambda b, i: (0,) * a.ndim, pipeline_mode=pl.Buffered(1))
    consts = (d_skip, w_glu, b_glu, w_att, w_ssm, w_out, g_post)
    return pl.pallas_call(
        functools.partial(_merge_kernel, tm=tm),
        grid=(B, L // tm),
        in_specs=[row(D_ATT),
                  pl.BlockSpec((1, N_GROUPS, tm // CHUNK, FLAT), lambda b, i: (b, 0, i, 0)),
                  row(D_SSM), row(D_MODEL), row(D_MODEL)] + [const(a) for a in consts],
        out_specs=row(D_MODEL),
        out_shape=jax.ShapeDtypeStruct((B, L, D_MODEL), F32),
        scratch_shapes=[pltpu.VMEM((D_SSM // LANES, tm, LANES), F32)],
        compiler_params=pltpu.CompilerParams(
            dimension_semantics=("arbitrary", "arbitrary"), vmem_limit_bytes=VMEM_LIMIT),
        name="merge",
    )(att, yf, u, ga, gs, *consts)


HALO = 8
FF_CHUNK = 256


def _ffn_kernel(xp_ref, x_ref, xn_ref, dp_ref, d_ref, dn_ref, gpre_ref, wup_ref, cw_ref, cb_ref, wdn_ref,
                gpost_ref, o_ref, *, tm, n_tiles):
    i = pl.program_id(1)
    x = x_ref[0] + d_ref[0]
    xp = jnp.where(i == 0, 0.0, xp_ref[0] + dp_ref[0])
    xn = jnp.where(i == n_tiles - 1, 0.0, xn_ref[0] + dn_ref[0])
    h = _rms(jnp.concatenate([xp, x, xn], axis=0), gpre_ref[...]).astype(BF16)
    n_ext = tm + 2 * HALO

    def conv(c0):
        up = jnp.dot(h, wup_ref[:, c0:c0 + FF_CHUNK], preferred_element_type=F32)
        w = cw_ref[:, c0:c0 + FF_CHUNK]
        prev = pltpu.roll(up, 1, 0)
        nxt = pltpu.roll(up, n_ext - 1, 0)
        r = prev * w[0:1] + up * w[1:2] + nxt * w[2:3] + cb_ref[:, c0:c0 + FF_CHUNK]
        return r[HALO:HALO + tm]

    acts = [(_gelu(conv(c * FF_CHUNK)) * conv(D_FF + c * FF_CHUNK)).astype(BF16)
            for c in range(D_FF // FF_CHUNK)]
    acc = jnp.dot(jnp.concatenate(acts, axis=1), wdn_ref[...], preferred_element_type=F32)
    o_ref[0] = x + _rms(acc, gpost_ref[...])


def _ffn(x, d, g_pre, w_up, conv_w, conv_b, w_down, g_post, *, tm=1024):
    B, L, _ = x.shape
    n_tiles = L // tm
    hb = tm // HALO
    const = lambda a: pl.BlockSpec(a.shape, lambda b, i: (0,) * a.ndim, pipeline_mode=pl.Buffered(1))
    consts_a = (g_pre, w_up, conv_w, conv_b, w_down, g_post)
    tiles = [pl.BlockSpec((1, HALO, D_MODEL), lambda b, i: (b, jnp.maximum(i * hb - 1, 0), 0)),
             pl.BlockSpec((1, tm, D_MODEL), lambda b, i: (b, i, 0)),
             pl.BlockSpec((1, HALO, D_MODEL), lambda b, i: (b, jnp.minimum((i + 1) * hb, L // HALO - 1), 0))]
    return pl.pallas_call(
        functools.partial(_ffn_kernel, tm=tm, n_tiles=n_tiles),
        grid=(B, n_tiles),
        in_specs=tiles + tiles + [const(a) for a in consts_a],
        out_specs=pl.BlockSpec((1, tm, D_MODEL), lambda b, i: (b, i, 0)),
        out_shape=jax.ShapeDtypeStruct((B, L, D_MODEL), F32),
        compiler_params=pltpu.CompilerParams(
            dimension_semantics=("arbitrary", "arbitrary"), vmem_limit_bytes=FFN_VMEM_LIMIT),
        name="ffn",
    )(x, x, x, d, d, d, *consts_a)


def _layer(x, p):
    q, k, v, u, uf, ga, gs = _inproj(x, p['g_mix_pre'], p['w_in'])
    att = _attention(q, k, v, p['attn_bias'])
    yf = _ssm(uf, p['pp'], p['m'], p['sq'], p['coef'])
    d = _merge(att, yf, u, ga, gs, p['ssm_d'], p['w_glu'], p['b_glu'], p['w_branch_att'],
               p['w_branch_ssm'], p['w_out'], p['g_mix_post'])
    return _ffn(x, d, p['g_ffn_pre'], p['w_up'], p['conv_w'], p['conv_b'], p['w_down'], p['g_ffn_post'])


def _prepare(l, g_mix_pre, g_mix_post, w_in, attn_rpb, ssm_lam_re, ssm_lam_im, ssm_log_dt, ssm_b_re,
             ssm_b_im, ssm_c_re, ssm_c_im, ssm_d, w_glu, b_glu, w_branch_att, w_branch_ssm, w_out,
             g_ffn_pre, g_ffn_post, w_up, conv_w, conv_b, w_down):
    pp, m, sq, coef = _ssm_tables(ssm_lam_re[l], ssm_lam_im[l], ssm_log_dt[l], ssm_b_re[l], ssm_b_im[l],
                               ssm_c_re[l], ssm_c_im[l])
    vec = lambda a: a[l].astype(F32).reshape(1, -1)
    return dict(
        g_mix_pre=vec(g_mix_pre), g_mix_post=vec(g_mix_post), w_in=w_in[l].astype(BF16),
        attn_bias=_attn_bias_blocks(attn_rpb[l].astype(F32)), pp=pp, m=m, sq=sq, coef=coef, ssm_d=vec(ssm_d),
        w_glu=w_glu[l].astype(BF16), b_glu=vec(b_glu), w_branch_att=w_branch_att[l].astype(BF16),
        w_branch_ssm=w_branch_ssm[l].astype(BF16), w_out=w_out[l].astype(BF16),
        g_ffn_pre=vec(g_ffn_pre), g_ffn_post=vec(g_ffn_post), w_up=w_up[l].astype(BF16),
        conv_w=conv_w[l].astype(F32), conv_b=vec(conv_b), w_down=w_down[l].astype(BF16))


def kernel(x_prompt, x_sample, g_mix_pre, g_mix_post, w_in, attn_rpb, ssm_lam_re, ssm_lam_im, ssm_log_dt,
           ssm_b_re, ssm_b_im, ssm_c_re, ssm_c_im, ssm_d, w_glu, b_glu, w_branch_att, w_branch_ssm, w_out,
           g_ffn_pre, g_ffn_post, w_up, conv_w, conv_b, w_down):
    weights = (g_mix_pre, g_mix_post, w_in, attn_rpb, ssm_lam_re, ssm_lam_im, ssm_log_dt, ssm_b_re,
               ssm_b_im, ssm_c_re, ssm_c_im, ssm_d, w_glu, b_glu, w_branch_att, w_branch_ssm, w_out,
               g_ffn_pre, g_ffn_post, w_up, conv_w, conv_b, w_down)
    layers = [_prepare(l, *weights) for l in range(w_in.shape[0])]

    def trunk(x):
        for p in layers:
            x = _layer(x, p)
        return x

    return trunk(x_prompt), trunk(x_sample)
```
